```python
import math
import jax, jax.numpy as jnp
from jax import lax
import numpy as np

D_MODEL = 2048
BATCH = 8
SEQ = 2048
DEPTH = 1

CHUNK = 64
Q_BLOCK = 128
D_SSM = D_MODEL // 2
SSD_HEADDIM = 64
SSD_HEADS = D_SSM // SSD_HEADDIM
SSD_GROUPS = 2
SSD_STATE = 128
SSD_CONV = 4
SSD_CONV_DIM = D_SSM + 2 * SSD_GROUPS * SSD_STATE
MLA_V = 128
MLA_HEADS = (D_MODEL - D_SSM) // MLA_V
MLA_NOPE = 128
MLA_ROPE = 64
MLA_Q_RANK = 512
MLA_KV_RANK = 256
ROPE_THETA = 10000.0
D_IN_PROJ = D_SSM + SSD_CONV_DIM + SSD_HEADS + MLA_Q_RANK + MLA_KV_RANK + MLA_ROPE
D_FF = 5632
FFN_CONV = 3
PLE_DIM = 256
NORM_EPS = 1e-6

kernel_name = "hybrid_ssd_mla_convffn_ple_block"


def rms_norm(x, w):
    xf = x.astype(jnp.float32)
    y = xf * lax.rsqrt(jnp.mean(xf * xf, axis=-1, keepdims=True) + NORM_EPS)
    return y.astype(x.dtype) * w


def causal_dwconv(u, w, b):
    k = w.shape[0]
    out = lax.conv_general_dilated(
        u, w[:, None, :].astype(u.dtype), window_strides=(1,), padding=[(k - 1, 0)],
        dimension_numbers=("NWC", "WIO", "NWC"), feature_group_count=u.shape[-1])
    return out + b


def ssd_chunked(xh, dt, a, bm, cm):
    bsz, s, g, r, pdim = xh.shape
    n = bm.shape[-1]
    nc = s // CHUNK
    xdt = (xh * dt[..., None]).reshape(bsz, nc, CHUNK, g, r, pdim)
    adt = jnp.moveaxis((dt * a).reshape(bsz, nc, CHUNK, g, r), (3, 4), (1, 2))
    a_cs = jnp.cumsum(adt, axis=-1)
    bc = bm.reshape(bsz, nc, CHUNK, g, n)
    cc = cm.reshape(bsz, nc, CHUNK, g, n)
    seg = a_cs[..., :, None] - a_cs[..., None, :]
    tri = jnp.tril(jnp.ones((CHUNK, CHUNK), dtype=bool))
    lmat = jnp.exp(jnp.where(tri, seg, -jnp.inf))
    y_diag = jnp.einsum("bclgn,bcsgn,bgrcls,bcsgrp->bclgrp", cc, bc, lmat, xdt)
    decay_states = jnp.exp(a_cs[..., -1:] - a_cs)
    states = jnp.einsum("bclgn,bgrcl,bclgrp->bcgrpn", bc, decay_states, xdt)
    chunk_decay = jnp.moveaxis(jnp.exp(a_cs[..., -1]), -1, 0)

    def step(h, inp):
        st, dec = inp
        return h * dec[..., None, None] + st, h

    h0 = jnp.zeros((bsz, g, r, pdim, n), jnp.float32)
    _, prev = lax.scan(step, h0, (jnp.moveaxis(states, 1, 0), chunk_decay))
    prev = jnp.moveaxis(prev, 0, 1)
    y_off = jnp.einsum("bclgn,bcgrpn,bgrcl->bclgrp", cc, prev, jnp.exp(a_cs))
    return (y_diag + y_off).reshape(bsz, s, g, r, pdim)


def ssd_mixer(z, xbc, dt_raw, conv_w, conv_b, dt_bias, a_log, d_skip, norm_w):
    bsz, s = z.shape[:2]
    r = SSD_HEADS // SSD_GROUPS
    xbc = jax.nn.silu(causal_dwconv(xbc, conv_w, conv_b))
    xs, bm, cm = jnp.split(xbc, [D_SSM, D_SSM + SSD_GROUPS * SSD_STATE], axis=-1)
    xh = xs.reshape(bsz, s, SSD_GROUPS, r, SSD_HEADDIM).astype(jnp.float32)
    dt = jax.nn.softplus(dt_raw.astype(jnp.float32) + dt_bias.astype(jnp.float32))
    dt = dt.reshape(bsz, s, SSD_GROUPS, r)
    a = -jnp.exp(a_log.astype(jnp.float32)).reshape(SSD_GROUPS, r)
    y = ssd_chunked(xh, dt, a,
                    bm.reshape(bsz, s, SSD_GROUPS, SSD_STATE).astype(jnp.float32),
                    cm.reshape(bsz, s, SSD_GROUPS, SSD_STATE).astype(jnp.float32))
    y = y + d_skip.astype(jnp.float32).reshape(SSD_GROUPS, r)[..., None] * xh
    y = y.reshape(bsz, s, D_SSM) * jax.nn.silu(z.astype(jnp.float32))
    yg = y.reshape(bsz, s, SSD_GROUPS, D_SSM // SSD_GROUPS)
    yg = yg * lax.rsqrt(jnp.mean(yg * yg, axis=-1, keepdims=True) + NORM_EPS)
    return yg.reshape(bsz, s, D_SSM).astype(z.dtype) * norm_w


def apply_rope(t, cos, sin):
    half = t.shape[-1] // 2
    t1, t2 = t[..., :half], t[..., half:]
    return jnp.concatenate([t1 * cos - t2 * sin, t1 * sin + t2 * cos], axis=-1).astype(t.dtype)


def mla_mixer(q_a, kv_a, cos, sin, q_a_norm_w, w_q_b, kv_a_norm_w, w_kv_b):
    bsz, s = q_a.shape[:2]
    q = (rms_norm(q_a, q_a_norm_w) @ w_q_b).reshape(bsz, s, MLA_HEADS, MLA_NOPE + MLA_ROPE)
    q_nope, q_rope = q[..., :MLA_NOPE], q[..., MLA_NOPE:]
    q_rope = apply_rope(q_rope, cos[:, :, None, :], sin[:, :, None, :])
    c_kv, k_rope = kv_a[..., :MLA_KV_RANK], kv_a[..., MLA_KV_RANK:]
    k_rope = apply_rope(k_rope, cos, sin)
    kv = (rms_norm(c_kv, kv_a_norm_w) @ w_kv_b).reshape(bsz, s, MLA_HEADS, MLA_NOPE + MLA_V)
    k_nope, v = kv[..., :MLA_NOPE], kv[..., MLA_NOPE:]
    scale = 1.0 / math.sqrt(MLA_NOPE + MLA_ROPE)
    nb = s // Q_BLOCK
    k_chunk = jnp.arange(s) // CHUNK

    def attend_block(args):
        qn, qr, start = args
        sc = (jnp.einsum("bqhd,bkhd->bhqk", qn, k_nope)
              + jnp.einsum("bqhr,bkr->bhqk", qr, k_rope)).astype(jnp.float32) * scale
        q_chunk = (start + jnp.arange(Q_BLOCK)) // CHUNK
        mask = k_chunk[None, :] <= q_chunk[:, None]
        sc = jnp.where(mask, sc, -jnp.inf)
        prob = jax.nn.softmax(sc, axis=-1).astype(v.dtype)
        return jnp.einsum("bhqk,bkhd->bqhd", prob, v)

    qn_b = jnp.moveaxis(q_nope.reshape(bsz, nb, Q_BLOCK, MLA_HEADS, MLA_NOPE), 1, 0)
    qr_b = jnp.moveaxis(q_rope.reshape(bsz, nb, Q_BLOCK, MLA_HEADS, MLA_ROPE), 1, 0)
    starts = jnp.arange(nb) * Q_BLOCK
    out = lax.map(attend_block, (qn_b, qr_b, starts))
    return jnp.moveaxis(out, 0, 1).reshape(bsz, s, MLA_HEADS * MLA_V)


def conv_ffn(h, w_up, conv_w, conv_b, w_down):
    u = causal_dwconv(h @ w_up, conv_w, conv_b)
    gate, up = u[..., :D_FF], u[..., D_FF:]
    return (jax.nn.silu(gate) * up) @ w_down


def setup_inputs(seed: int = 0) -> dict:
    key = jax.random.key(seed)
    ks = jax.random.split(key, 32)
    f32 = jnp.float32

    def nrm(k, shape, scale):
        return jax.random.normal(k, shape, f32) * scale

    def gain(k, n):
        return 1.0 + 0.02 * jax.random.normal(k, (DEPTH, n), f32)

    x = jax.random.normal(ks[0], (BATCH, SEQ, D_MODEL), f32)
    p = jax.random.normal(ks[1], (DEPTH, BATCH, SEQ, PLE_DIM), f32)
    offsets = jax.random.randint(ks[2], (BATCH, 1), 0, 64) * CHUNK
    positions = (offsets + jnp.arange(SEQ)[None, :]).astype(jnp.int32)
    dt0 = jnp.exp(jax.random.uniform(ks[3], (DEPTH, SSD_HEADS), f32)
                  * (math.log(0.1) - math.log(0.001)) + math.log(0.001))
    dt_bias = dt0 + jnp.log(-jnp.expm1(-dt0))
    a_log = jnp.log(jax.random.uniform(ks[4], (DEPTH, SSD_HEADS), f32, 1.0, 16.0))
    return {
        "x": x,
        "p": p,
        "positions": positions,
        "mix_norm_w": gain(ks[5], D_MODEL),
        "w_in": nrm(ks[6], (DEPTH, D_MODEL, D_IN_PROJ), D_MODEL ** -0.5),
        "conv_w": nrm(ks[7], (DEPTH, SSD_CONV, SSD_CONV_DIM), SSD_CONV ** -0.5),
        "conv_b": nrm(ks[8], (DEPTH, SSD_CONV_DIM), 0.01),
        "dt_bias": dt_bias,
        "a_log": a_log,
        "d_skip": 1.0 + 0.1 * jax.random.normal(ks[9], (DEPTH, SSD_HEADS), f32),
        "ssd_norm_w": gain(ks[10], D_SSM),
        "q_a_norm_w": gain(ks[11], MLA_Q_RANK),
        "w_q_b": nrm(ks[12], (DEPTH, MLA_Q_RANK, MLA_HEADS * (MLA_NOPE + MLA_ROPE)), MLA_Q_RANK ** -0.5),
        "kv_a_norm_w": gain(ks[13], MLA_KV_RANK),
        "w_kv_b": nrm(ks[14], (DEPTH, MLA_KV_RANK, MLA_HEADS * (MLA_NOPE + MLA_V)), MLA_KV_RANK ** -0.5),
        "w_out": nrm(ks[15], (DEPTH, D_SSM + MLA_HEADS * MLA_V, D_MODEL), D_MODEL ** -0.5),
        "ffn_norm_w": gain(ks[16], D_MODEL),
        "w_ffn_up": nrm(ks[17], (DEPTH, D_MODEL, 2 * D_FF), D_MODEL ** -0.5),
        "ffn_conv_w": nrm(ks[18], (DEPTH, FFN_CONV, 2 * D_FF), FFN_CONV ** -0.5),
        "ffn_conv_b": nrm(ks[19], (DEPTH, 2 * D_FF), 0.01),
        "w_ffn_down": nrm(ks[20], (DEPTH, D_FF, D_MODEL), D_FF ** -0.5),
        "ple_norm_w": gain(ks[21], D_MODEL),
        "w_ple_gate": nrm(ks[22], (DEPTH, D_MODEL, D_MODEL), D_MODEL ** -0.5),
        "b_ple_gate": nrm(ks[23], (DEPTH, D_MODEL), 0.01),
        "w_ple_proj": nrm(ks[24], (DEPTH, PLE_DIM, D_MODEL), PLE_DIM ** -0.5),
        "ple_post_norm_w": gain(ks[25], D_MODEL),
        "final_norm_w": 1.0 + 0.02 * jax.random.normal(ks[26], (D_MODEL,), f32),
    }


def reference(x, p, positions, mix_norm_w, w_in, conv_w, conv_b, dt_bias, a_log, d_skip,
              ssd_norm_w, q_a_norm_w, w_q_b, kv_a_norm_w, w_kv_b, w_out, ffn_norm_w,
              w_ffn_up, ffn_conv_w, ffn_conv_b, w_ffn_down, ple_norm_w, w_ple_gate,
              b_ple_gate, w_ple_proj, ple_post_norm_w, final_norm_w):
    inv_freq = ROPE_THETA ** (-jnp.arange(0, MLA_ROPE, 2, dtype=jnp.float32) / MLA_ROPE)
    ang = positions.astype(jnp.float32)[..., None] * inv_freq
    cos, sin = jnp.cos(ang), jnp.sin(ang)
    splits = [D_SSM, D_SSM + SSD_CONV_DIM, D_SSM + SSD_CONV_DIM + SSD_HEADS,
              D_SSM + SSD_CONV_DIM + SSD_HEADS + MLA_Q_RANK]
    for i in range(DEPTH):
        h = rms_norm(x, mix_norm_w[i])
        z, xbc, dt_raw, q_a, kv_a = jnp.split(h @ w_in[i], splits, axis=-1)
        y_ssd = ssd_mixer(z, xbc, dt_raw, conv_w[i], conv_b[i], dt_bias[i], a_log[i],
                          d_skip[i], ssd_norm_w[i])
        y_mla = mla_mixer(q_a, kv_a, cos, sin, q_a_norm_w[i], w_q_b[i],
                          kv_a_norm_w[i], w_kv_b[i])
        x = x + jnp.concatenate([y_ssd, y_mla], axis=-1) @ w_out[i]
        x = x + conv_ffn(rms_norm(x, ffn_norm_w[i]), w_ffn_up[i], ffn_conv_w[i],
                         ffn_conv_b[i], w_ffn_down[i])
        gate = jax.nn.sigmoid(rms_norm(x, ple_norm_w[i]) @ w_ple_gate[i] + b_ple_gate[i])
        x = x + gate * rms_norm(p[i] @ w_ple_proj[i], ple_post_norm_w[i])
    return rms_norm(x, final_norm_w)
```

```python
import functools
import math

import jax
import jax.numpy as jnp
from jax import lax
from jax.experimental import pallas as pl
from jax.experimental.pallas import tpu as pltpu

F32 = jnp.float32
BF16 = jnp.bfloat16

D_MODEL = 2048
CHUNK = 64
D_SSM = 1024
SSD_HEADDIM = 64
SSD_HEADS = 16
SSD_GROUPS = 2
SSD_STATE = 128
SSD_CONV = 4
SSD_CONV_DIM = D_SSM + 2 * SSD_GROUPS * SSD_STATE
GROUP_W = D_SSM // SSD_GROUPS
MLA_V = 128
MLA_HEADS = 8
MLA_NOPE = 128
MLA_ROPE = 64
MLA_Q_RANK = 512
MLA_KV_RANK = 256
ROPE_THETA = 10000.0
D_FF = 5632
FFN_CONV = 3
PLE_DIM = 256
NORM_EPS = 1e-6

LANES = 128
SUBLANES = 8
QK_W = 2 * LANES
VMEM_LIMIT = 56 * 1024 * 1024

IN_Z = 0
IN_XBC = IN_Z + D_SSM
IN_QA = IN_XBC + SSD_CONV_DIM
IN_CKV = IN_QA + MLA_Q_RANK
IN_KR = IN_CKV + MLA_KV_RANK
IN_DT = IN_KR + LANES
IN_N = IN_DT + LANES
DT_COPIES = 3


def _rms(xf):
    return xf * lax.rsqrt(jnp.mean(xf * xf, axis=-1, keepdims=True) + NORM_EPS)


def _silu(v):
    return v * jax.nn.sigmoid(v)


def _dot(a, b):
    return jnp.dot(a, b, preferred_element_type=F32)


def _dot_nt(a, b):
    return lax.dot_general(a, b, (((1,), (1,)), ((), ())), preferred_element_type=F32)


def _const_spec(shape):
    nd = len(shape)
    return pl.BlockSpec(shape, lambda *_: (0,) * nd, pipeline_mode=pl.Buffered(1))


def _params(sem, vmem=VMEM_LIMIT):
    return pltpu.CompilerParams(dimension_semantics=sem, vmem_limit_bytes=vmem)


def _rope_table_kernel(pos_ref, freq_ref, cos_ref, sin_ref):
    ang = pos_ref[...].astype(F32) * freq_ref[...]
    cos_ref[...] = jnp.cos(ang)
    sin_ref[...] = jnp.sin(ang)


def _rope_tables(positions):
    t = positions.size
    half = MLA_ROPE // 2
    inv_freq = ROPE_THETA ** (-jnp.arange(0, MLA_ROPE, 2, dtype=F32) / MLA_ROPE)
    rows = t * half // LANES
    pos_rep = jnp.repeat(positions.reshape(t), half).reshape(rows, LANES)
    freq = jnp.tile(inv_freq, LANES // half).reshape(1, LANES)
    cos, sin = pl.pallas_call(
        _rope_table_kernel,
        out_shape=(jax.ShapeDtypeStruct((rows, LANES), F32),) * 2,
        name="rope_table",
    )(pos_rep, freq)
    cos = cos.reshape(t, half)
    sin = sin.reshape(t, half)
    zeros = jnp.zeros((t, LANES - MLA_ROPE), F32)
    c2 = jnp.concatenate([cos, cos, zeros], axis=-1)
    s2 = jnp.concatenate([-sin, sin, zeros], axis=-1)
    return c2, s2


def _rope(r, c2, s2):
    half = MLA_ROPE // 2
    lane = lax.broadcasted_iota(jnp.int32, r.shape, 1)
    swapped = jnp.where(lane < half, pltpu.roll(r, LANES - half, 1), pltpu.roll(r, half, 1))
    return r * c2 + swapped * s2


def _in_proj_kernel(x_ref, nw_ref, w_ref, z_ref, xbc_ref, qa_ref, ckv_ref, kr_ref, dt_ref):
    h = (_rms(x_ref[...]) * nw_ref[...]).astype(BF16)

    def mm(lo, hi):
        return _dot(h, w_ref[:, lo:hi])

    z_ref[...] = mm(IN_Z, IN_XBC).astype(BF16)
    xbc_ref[...] = mm(IN_XBC, IN_QA).astype(BF16)
    qa_ref[...] = mm(IN_QA, IN_CKV).astype(BF16)
    ckv_ref[...] = mm(IN_CKV, IN_KR).astype(BF16)
    kr_ref[...] = mm(IN_KR, IN_DT)
    dt_ref[...] = mm(IN_DT, IN_N)


def _in_proj(x2d, norm_w, w_in_r, tm):
    t = x2d.shape[0]
    widths = (D_SSM, SSD_CONV_DIM, MLA_Q_RANK, MLA_KV_RANK, LANES, LANES)
    dtypes = (BF16, BF16, BF16, BF16, F32, F32)
    return pl.pallas_call(
        _in_proj_kernel,
        grid=(t // tm,),
        in_specs=[
            pl.BlockSpec((tm, D_MODEL), lambda i: (i, 0)),
            _const_spec((1, D_MODEL)),
            _const_spec((D_MODEL, IN_N)),
        ],
        out_specs=[pl.BlockSpec((tm, w), lambda i: (i, 0)) for w in widths],
        out_shape=[jax.ShapeDtypeStruct((t, w), d) for w, d in zip(widths, dtypes)],
        compiler_params=_params(("parallel",)),
        name="in_proj",
    )(x2d, norm_w, w_in_r)


def _ssd_kernel(z_ref, xbc_ref, dt_ref, cw_ref, cb_ref, dtb_ref, alog_ref, dsk_ref, nw_ref,
                cum_ref, e3_ref, e1_ref, y_ref, xpad_ref, state_ref, yacc_ref, *, ts):
    blk = pl.program_id(1)
    halo = SUBLANES

    @pl.when(blk == 0)
    def _():
        xpad_ref[0:halo, :] = jnp.zeros((halo, SSD_CONV_DIM), F32)
        state_ref[...] = jnp.zeros_like(state_ref)

    xpad_ref[halo:halo + ts, :] = xbc_ref[...].astype(F32)
    conv = cb_ref[...]
    for k in range(SSD_CONV):
        off = halo - (SSD_CONV - 1) + k
        conv = conv + cw_ref[k:k + 1, :] * xpad_ref[off:off + ts, :]
    xpad_ref[0:halo, :] = xpad_ref[ts:ts + halo, :]
    xc = _silu(conv)
    xs = xc[:, :D_SSM]
    bm = xc[:, D_SSM:D_SSM + SSD_GROUPS * SSD_STATE]
    cm = xc[:, D_SSM + SSD_GROUPS * SSD_STATE:]

    lane = lax.broadcasted_iota(jnp.int32, (ts, LANES), 1)
    raw = dt_ref[...] + dtb_ref[...]
    dt = jnp.maximum(raw, 0.0) + jnp.log1p(jnp.exp(-jnp.abs(raw)))
    a = jnp.where(lane < DT_COPIES * SSD_HEADS, -jnp.exp(alog_ref[...]), 0.0)
    adt = dt * a
    cum = jnp.dot(cum_ref[...], adt, preferred_element_type=F32, precision=lax.Precision.HIGHEST)
    acs, tot = cum[:ts], cum[ts:]
    eacs = jnp.exp(acs)
    dtdec = dt * jnp.exp(tot - acs)

    stacked = jnp.concatenate([dt, eacs, dtdec], axis=0).astype(BF16)
    expanded = _dot(stacked, e1_ref[...])
    dt_e, eacs_e, dtdec_e = expanded[:ts], expanded[ts:2 * ts], expanded[2 * ts:]

    hi = acs.astype(BF16).astype(F32)
    rem = acs - hi
    mid = rem.astype(BF16).astype(F32)
    lo = rem - mid
    split = jnp.where(lane < SSD_HEADS, hi, jnp.where(lane < 2 * SSD_HEADS, mid, lo)).astype(BF16)
    col_part = _dot(split, e3_ref[...])

    xdt = (xs * dt_e).astype(BF16)
    xdec = (xs * dtdec_e).astype(BF16)
    bm16 = bm.astype(BF16)
    cm16 = cm.astype(BF16)

    li = lax.broadcasted_iota(jnp.int32, (CHUNK, D_SSM), 0)
    si = lax.broadcasted_iota(jnp.int32, (CHUNK, D_SSM), 1) % CHUNK
    heads_per_mm = 2 * LANES // SSD_HEADDIM
    bw = heads_per_mm * SSD_HEADDIM
    bd_r = lax.broadcasted_iota(jnp.int32, (bw, bw), 0) // CHUNK
    bd_c = lax.broadcasted_iota(jnp.int32, (bw, bw), 1) // SSD_HEADDIM
    bd_mask = bd_r == bd_c

    for c in range(ts // CHUNK):
        r0 = c * CHUNK
        rows = slice(r0, r0 + CHUNK)
        cp = col_part[rows]
        row_part = jnp.sum(jnp.where(li == si, cp, 0.0), axis=0, keepdims=True)
        lmat = jnp.exp(jnp.where(li >= si, cp - row_part, -jnp.inf))

        pieces = []
        for g in range(SSD_GROUPS):
            gl = slice(g * SSD_STATE, (g + 1) * SSD_STATE)
            gw = slice(g * GROUP_W, (g + 1) * GROUP_W)
            cg = cm16[rows, gl]
            bg = bm16[rows, gl]
            heads_g = GROUP_W // CHUNK
            cb_t = _dot_nt(cg, jnp.concatenate([bg] * heads_g, axis=0))
            m_g = (cb_t * lmat[:, gw]).astype(BF16)
            y_off = _dot(cg, state_ref[g].astype(BF16)) * eacs_e[rows, gw]
            y_diag = []
            for j in range(GROUP_W // bw):
                cols = slice(g * GROUP_W + j * bw, g * GROUP_W + (j + 1) * bw)
                xj = xdt[rows, cols]
                bd = jnp.where(bd_mask, jnp.concatenate([xj] * heads_per_mm, axis=0), jnp.zeros((), BF16))
                y_diag.append(_dot(m_g[:, j * bw:(j + 1) * bw], bd))
            pieces.append(jnp.concatenate(y_diag, axis=1) + y_off)
            bg_t = bm[rows, gl].T.astype(BF16)
            new_state = _dot(bg_t, xdec[rows, gw])
            decay = eacs_e[r0 + CHUNK - 1:r0 + CHUNK, gw]
            state_ref[g] = state_ref[g] * decay + new_state
        yacc_ref[rows, :] = jnp.concatenate(pieces, axis=1)

    y = yacc_ref[...] + dsk_ref[...] * xs
    y = y * _silu(z_ref[...].astype(F32))
    outs = []
    for g in range(SSD_GROUPS):
        outs.append(_rms(y[:, g * GROUP_W:(g + 1) * GROUP_W]))
    y_ref[...] = (jnp.concatenate(outs, axis=1) * nw_ref[...]).astype(BF16)


def _ssd(z, xbc, dt, conv_w, conv_b, dtb, alog, dsk, norm_w, batch, seq, ts):
    nblk = seq // ts
    r = jnp.arange(ts)
    same = (r[:, None] // CHUNK) == (r[None, :] // CHUNK)
    cum_mat = jnp.concatenate([same & (r[:, None] >= r[None, :]), same], axis=0).astype(F32)
    lane_head = jnp.arange(D_SSM) // SSD_HEADDIM
    e_rows = (jnp.arange(LANES)[:, None] % SSD_HEADS) == lane_head[None, :]
    e1 = (e_rows & (jnp.arange(LANES)[:, None] < SSD_HEADS)).astype(BF16)
    e3 = (e_rows & (jnp.arange(LANES)[:, None] < DT_COPIES * SSD_HEADS)).astype(BF16)

    def row_spec(w):
        return pl.BlockSpec((ts, w), lambda b, j: (b * nblk + j, 0))

    return pl.pallas_call(
        functools.partial(_ssd_kernel, ts=ts),
        grid=(batch, nblk),
        in_specs=[
            row_spec(D_SSM), row_spec(SSD_CONV_DIM), row_spec(LANES),
            _const_spec((SSD_CONV, SSD_CONV_DIM)), _const_spec((1, SSD_CONV_DIM)),
            _const_spec((1, LANES)), _const_spec((1, LANES)),
            _const_spec((1, D_SSM)), _const_spec((1, D_SSM)),
            _const_spec((2 * ts, ts)), _const_spec((LANES, D_SSM)), _const_spec((LANES, D_SSM)),
        ],
        out_specs=row_spec(D_SSM),
        out_shape=jax.ShapeDtypeStruct((batch * seq, D_SSM), BF16),
        scratch_shapes=[
            pltpu.VMEM((ts + SUBLANES, SSD_CONV_DIM), F32),
            pltpu.VMEM((SSD_GROUPS, SSD_STATE, GROUP_W), F32),
            pltpu.VMEM((ts, D_SSM), F32),
        ],
        compiler_params=_params(("parallel", "arbitrary")),
        name="ssd",
    )(z, xbc, dt, conv_w, conv_b, dtb, alog, dsk, norm_w, cum_mat, e3, e1)


def _mla_proj_kernel(qa_ref, ckv_ref, kr_ref, c2_ref, s2_ref, qnw_ref, wq_ref, kvnw_ref, wkv_ref,
                     q_ref, k_ref, v_ref):
    c2 = c2_ref[...]
    s2 = s2_ref[...]
    scale = 1.0 / math.sqrt(MLA_NOPE + MLA_ROPE)
    qn = (_rms(qa_ref[...].astype(F32)) * qnw_ref[...]).astype(BF16)
    q = _dot(qn, wq_ref[...]) * scale
    cn = (_rms(ckv_ref[...].astype(F32)) * kvnw_ref[...]).astype(BF16)
    kv = _dot(cn, wkv_ref[...])
    k_rope = _rope(kr_ref[...], c2, s2).astype(BF16)
    for h in range(MLA_HEADS):
        base = h * QK_W
        q_ref[:, base:base + LANES] = q[:, base:base + LANES].astype(BF16)
        q_ref[:, base + LANES:base + QK_W] = _rope(q[:, base + LANES:base + QK_W], c2, s2).astype(BF16)
        k_ref[:, base:base + LANES] = kv[:, h * MLA_NOPE:(h + 1) * MLA_NOPE].astype(BF16)
        k_ref[:, base + LANES:base + QK_W] = k_rope
    v_ref[...] = kv[:, MLA_HEADS * MLA_NOPE:].astype(BF16)


def _mla_proj(qa, ckv, kr, c2, s2, qnw, wq_r, kvnw, wkv_r, tm):
    t = qa.shape[0]

    def row_spec(w):
        return pl.BlockSpec((tm, w), lambda i: (i, 0))

    widths = (MLA_HEADS * QK_W, MLA_HEADS * QK_W, MLA_HEADS * MLA_V)
    return pl.pallas_call(
        _mla_proj_kernel,
        grid=(t // tm,),
        in_specs=[
            row_spec(MLA_Q_RANK), row_spec(MLA_KV_RANK), row_spec(LANES), row_spec(LANES), row_spec(LANES),
            _const_spec((1, MLA_Q_RANK)), _const_spec((MLA_Q_RANK, MLA_HEADS * QK_W)),
            _const_spec((1, MLA_KV_RANK)), _const_spec((MLA_KV_RANK, MLA_HEADS * (MLA_NOPE + MLA_V))),
        ],
        out_specs=[row_spec(w) for w in widths],
        out_shape=[jax.ShapeDtypeStruct((t, w), BF16) for w in widths],
        compiler_params=_params(("parallel",)),
        name="mla_proj",
    )(qa, ckv, kr, c2, s2, qnw, wq_r, kvnw, wkv_r)


def _attention_kernel(q_ref, k_ref, v_ref, o_ref, *, seq, tq):
    row_chunk = lax.broadcasted_iota(jnp.int32, (tq, tq), 0) // CHUNK
    col_chunk = lax.broadcasted_iota(jnp.int32, (tq, tq), 1) // CHUNK
    diag_mask = col_chunk <= row_chunk
    for i in range(seq // tq):
        lo = i * tq
        qi = q_ref[lo:lo + tq, :]
        s_diag = jnp.where(diag_mask, _dot_nt(qi, k_ref[lo:lo + tq, :]), -jnp.inf)
        m = jnp.max(s_diag, axis=-1, keepdims=True)
        if i > 0:
            s_off = _dot_nt(qi, k_ref[0:lo, :])
            m = jnp.maximum(m, jnp.max(s_off, axis=-1, keepdims=True))
        p_diag = jnp.exp(s_diag - m)
        denom = jnp.sum(p_diag, axis=-1, keepdims=True)
        acc = _dot(p_diag.astype(BF16), v_ref[lo:lo + tq, :])
        if i > 0:
            p_off = jnp.exp(s_off - m)
            denom = denom + jnp.sum(p_off, axis=-1, keepdims=True)
            acc = acc + _dot(p_off.astype(BF16), v_ref[0:lo, :])
        o_ref[lo:lo + tq, :] = (acc / denom).astype(BF16)


def _attention(q, k, v, batch, seq, tq):
    q3 = q.reshape(batch, seq, MLA_HEADS * QK_W)
    k3 = k.reshape(batch, seq, MLA_HEADS * QK_W)
    v3 = v.reshape(batch, seq, MLA_HEADS * MLA_V)

    def spec(w):
        return pl.BlockSpec((None, seq, w), lambda b, h: (b, 0, h))

    out = pl.pallas_call(
        functools.partial(_attention_kernel, seq=seq, tq=tq),
        grid=(batch, MLA_HEADS),
        in_specs=[spec(QK_W), spec(QK_W), spec(MLA_V)],
        out_specs=spec(MLA_V),
        out_shape=jax.ShapeDtypeStruct((batch, seq, MLA_HEADS * MLA_V), BF16),
        compiler_params=_params(("parallel", "parallel")),
        name="attention",
    )(q3, k3, v3)
    return out.reshape(batch * seq, MLA_HEADS * MLA_V)


def _out_proj_kernel(x_ref, ys_ref, ym_ref, ws_ref, wm_ref, o_ref):
    o_ref[...] = x_ref[...] + _dot(ys_ref[...], ws_ref[...]) + _dot(ym_ref[...], wm_ref[...])


def _out_proj(x2d, y_ssd, y_mla, w_s, w_m, tm):
    t = x2d.shape[0]
    return pl.pallas_call(
        _out_proj_kernel,
        grid=(t // tm,),
        in_specs=[
            pl.BlockSpec((tm, D_MODEL), lambda i: (i, 0)),
            pl.BlockSpec((tm, D_SSM), lambda i: (i, 0)),
            pl.BlockSpec((tm, MLA_HEADS * MLA_V), lambda i: (i, 0)),
            _const_spec((D_SSM, D_MODEL)),
            _const_spec((MLA_HEADS * MLA_V, D_MODEL)),
        ],
        out_specs=pl.BlockSpec((tm, D_MODEL), lambda i: (i, 0)),
        out_shape=jax.ShapeDtypeStruct((t, D_MODEL), F32),
        compiler_params=_params(("parallel",)),
        name="out_proj",
    )(x2d, y_ssd, y_mla, w_s, w_m)


def _conv_ffn_kernel(x_ref, nw_ref, wg_ref, wu_ref, cwg_ref, cwu_ref, cbg_ref, cbu_ref, wd_ref, o_ref,
                     h_ref, acc_ref, gpad_ref, upad_ref, ghalo_ref, uhalo_ref, *, tm, tiles_per_seq):
    i = pl.program_id(0)
    f = pl.program_id(1)
    nf = pl.num_programs(1)
    halo = SUBLANES

    @pl.when(f == 0)
    def _():
        h_ref[...] = (_rms(x_ref[...]) * nw_ref[...]).astype(BF16)
        acc_ref[...] = jnp.zeros_like(acc_ref)

    h = h_ref[...]
    seq_start = (i % tiles_per_seq) == 0

    def conv(pad_ref, halo_ref, w_ref, cw_ref, cb_ref):
        prev = halo_ref[f]
        pad_ref[0:halo, :] = jnp.where(seq_start, jnp.zeros_like(prev), prev)
        pad_ref[halo:halo + tm, :] = _dot(h, w_ref[...])
        halo_ref[f] = pad_ref[tm:tm + halo, :]
        out = cb_ref[...]
        for k in range(FFN_CONV):
            off = halo - (FFN_CONV - 1) + k
            out = out + cw_ref[k:k + 1, :] * pad_ref[off:off + tm, :]
        return out

    gate = conv(gpad_ref, ghalo_ref, wg_ref, cwg_ref, cbg_ref)
    up = conv(upad_ref, uhalo_ref, wu_ref, cwu_ref, cbu_ref)
    act = (_silu(gate) * up).astype(BF16)
    acc_ref[...] += _dot(act, wd_ref[...])

    @pl.when(f == nf - 1)
    def _():
        o_ref[...] = x_ref[...] + acc_ref[...]


def _conv_ffn(x2d, norm_w, w_up, conv_w, conv_b, w_down, seq, tm, tf):
    t = x2d.shape[0]
    nf = D_FF // tf
    return pl.pallas_call(
        functools.partial(_conv_ffn_kernel, tm=tm, tiles_per_seq=seq // tm),
        grid=(t // tm, nf),
        in_specs=[
            pl.BlockSpec((tm, D_MODEL), lambda i, f: (i, 0)),
            _const_spec((1, D_MODEL)),
            pl.BlockSpec((D_MODEL, tf), lambda i, f: (0, f)),
            pl.BlockSpec((D_MODEL, tf), lambda i, f: (0, f + nf)),
            pl.BlockSpec((FFN_CONV, tf), lambda i, f: (0, f)),
            pl.BlockSpec((FFN_CONV, tf), lambda i, f: (0, f + nf)),
            pl.BlockSpec((1, tf), lambda i, f: (0, f)),
            pl.BlockSpec((1, tf), lambda i, f: (0, f + nf)),
            pl.BlockSpec((tf, D_MODEL), lambda i, f: (f, 0)),
        ],
        out_specs=pl.BlockSpec((tm, D_MODEL), lambda i, f: (i, 0)),
        out_shape=jax.ShapeDtypeStruct((t, D_MODEL), F32),
        scratch_shapes=[
            pltpu.VMEM((tm, D_MODEL), BF16),
            pltpu.VMEM((tm, D_MODEL), F32),
            pltpu.VMEM((tm + SUBLANES, tf), F32),
            pltpu.VMEM((tm + SUBLANES, tf), F32),
            pltpu.VMEM((nf, SUBLANES, tf), F32),
            pltpu.VMEM((nf, SUBLANES, tf), F32),
        ],
        compiler_params=_params(("arbitrary", "arbitrary")),
        name="conv_ffn",
    )(x2d, norm_w, w_up, w_up, conv_w, conv_w, conv_b, conv_b, w_down)


def _ple_final_kernel(x_ref, p_ref, nw_ref, wg_ref, bg_ref, wp_ref, pnw_ref, fnw_ref, o_ref):
    x = x_ref[...]
    hn = (_rms(x) * nw_ref[...]).astype(BF16)
    gate = jax.nn.sigmoid(_dot(hn, wg_ref[...]) + bg_ref[...])
    proj = _dot(p_ref[...].astype(BF16), wp_ref[...])
    x = x + gate * (_rms(proj) * pnw_ref[...])
    o_ref[...] = _rms(x) * fnw_ref[...]


def _ple_final(x2d, p2d, norm_w, w_gate, b_gate, w_proj, post_w, final_w, tm):
    t = x2d.shape[0]
    return pl.pallas_call(
        _ple_final_kernel,
        grid=(t // tm,),
        in_specs=[
            pl.BlockSpec((tm, D_MODEL), lambda i: (i, 0)),
            pl.BlockSpec((tm, PLE_DIM), lambda i: (i, 0)),
            _const_spec((1, D_MODEL)),
            _const_spec((D_MODEL, D_MODEL)),
            _const_spec((1, D_MODEL)),
            _const_spec((PLE_DIM, D_MODEL)),
            _const_spec((1, D_MODEL)),
            _const_spec((1, D_MODEL)),
        ],
        out_specs=pl.BlockSpec((tm, D_MODEL), lambda i: (i, 0)),
        out_shape=jax.ShapeDtypeStruct((t, D_MODEL), F32),
        compiler_params=_params(("parallel",)),
        name="ple_final",
    )(x2d, p2d, norm_w, w_gate, b_gate, w_proj, post_w, final_w)


def _pad_cols(w, width):
    return jnp.pad(w, ((0, 0), (0, width - w.shape[1])))


def _prep_w_in(w_in):
    o_xbc = D_SSM
    o_dt = o_xbc + SSD_CONV_DIM
    o_qa = o_dt + SSD_HEADS
    o_kv = o_qa + MLA_Q_RANK
    o_kr = o_kv + MLA_KV_RANK
    w_dt = w_in[:, o_dt:o_qa]
    return jnp.concatenate([
        w_in[:, :o_dt],
        w_in[:, o_qa:o_kr],
        _pad_cols(w_in[:, o_kr:], LANES),
        _pad_cols(jnp.concatenate([w_dt] * DT_COPIES, axis=1), LANES),
    ], axis=1).astype(BF16)


def _prep_w_q(w_q_b):
    w = w_q_b.reshape(MLA_Q_RANK, MLA_HEADS, MLA_NOPE + MLA_ROPE)
    w = jnp.pad(w, ((0, 0), (0, 0), (0, QK_W - MLA_NOPE - MLA_ROPE)))
    return w.reshape(MLA_Q_RANK, MLA_HEADS * QK_W).astype(BF16)


def _prep_w_kv(w_kv_b):
    w = w_kv_b.reshape(MLA_KV_RANK, MLA_HEADS, MLA_NOPE + MLA_V)
    w_k = w[:, :, :MLA_NOPE].reshape(MLA_KV_RANK, MLA_HEADS * MLA_NOPE)
    w_v = w[:, :, MLA_NOPE:].reshape(MLA_KV_RANK, MLA_HEADS * MLA_V)
    return jnp.concatenate([w_k, w_v], axis=1).astype(BF16)


def _lane_row(v, width=None):
    v = v.reshape(1, -1).astype(F32)
    return v if width is None else _pad_cols(v, width)


def kernel(x, p, positions, mix_norm_w, w_in, conv_w, conv_b, dt_bias, a_log, d_skip, ssd_norm_w,
           q_a_norm_w, w_q_b, kv_a_norm_w, w_kv_b, w_out, ffn_norm_w, w_ffn_up, ffn_conv_w,
           ffn_conv_b, w_ffn_down, ple_norm_w, w_ple_gate, b_ple_gate, w_ple_proj, ple_post_norm_w,
           final_norm_w):
    batch, seq, _ = x.shape
    depth = w_in.shape[0]
    t = batch * seq
    x2d = x.reshape(t, D_MODEL)
    c2, s2 = _rope_tables(positions)

    assert depth == 1, "the final RMSNorm is fused into the single layer's last kernel"
    i = 0
    z, xbc, qa, ckv, kr, dt = _in_proj(x2d, _lane_row(mix_norm_w[i]), _prep_w_in(w_in[i]), tm=512)
    y_ssd = _ssd(
        z, xbc, dt, conv_w[i], _lane_row(conv_b[i]),
        _lane_row(jnp.tile(dt_bias[i], DT_COPIES), LANES),
        _lane_row(jnp.tile(a_log[i], DT_COPIES), LANES),
        _lane_row(jnp.repeat(d_skip[i], SSD_HEADDIM)), _lane_row(ssd_norm_w[i]),
        batch, seq, ts=256)
    q, k, v = _mla_proj(qa, ckv, kr, c2, s2, _lane_row(q_a_norm_w[i]), _prep_w_q(w_q_b[i]),
                        _lane_row(kv_a_norm_w[i]), _prep_w_kv(w_kv_b[i]), tm=512)
    y_mla = _attention(q, k, v, batch, seq, tq=256)
    w_o = w_out[i].astype(BF16)
    x2d = _out_proj(x2d, y_ssd, y_mla, w_o[:D_SSM], w_o[D_SSM:], tm=512)
    x2d = _conv_ffn(x2d, _lane_row(ffn_norm_w[i]), w_ffn_up[i].astype(BF16), ffn_conv_w[i],
                    _lane_row(ffn_conv_b[i]), w_ffn_down[i].astype(BF16), seq, tm=512, tf=512)
    x2d = _ple_final(x2d, p[i].reshape(t, PLE_DIM), _lane_row(ple_norm_w[i]),
                     w_ple_gate[i].astype(BF16), _lane_row(b_ple_gate[i]), w_ple_proj[i].astype(BF16),
                     _lane_row(ple_post_norm_w[i]), _lane_row(final_norm_w), tm=512)
    return x2d.reshape(batch, seq, D_MODEL)
```

```python
import functools
import math

import jax
import jax.numpy as jnp
from jax import lax
from jax.experimental import pallas as pl
from jax.experimental.pallas import tpu as pltpu

F32 = jnp.float32
BF16 = jnp.bfloat16

D_MODEL = 2048
CHUNK = 64
D_SSM = 1024
SSD_HEADDIM = 64
SSD_HEADS = 16
SSD_GROUPS = 2
SSD_STATE = 128
SSD_CONV = 4
SSD_CONV_DIM = D_SSM + 2 * SSD_GROUPS * SSD_STATE
GROUP_W = D_SSM // SSD_GROUPS
MLA_V = 128
MLA_HEADS = 8
MLA_NOPE = 128
MLA_ROPE = 64
MLA_Q_RANK = 512
MLA_KV_RANK = 256
ROPE_THETA = 10000.0
D_FF = 5632
FFN_CONV = 3
PLE_DIM = 256
NORM_EPS = 1e-6

LANES = 128
SUBLANES = 8
QK_W = 2 * LANES
VMEM_LIMIT = 56 * 1024 * 1024
ROW_SPLIT = 2
FFN_ROW_CHUNKS = 2
FFN_STRIP_ROWS = 64

IN_Z = 0
IN_XBC = IN_Z + D_SSM
IN_QA = IN_XBC + SSD_CONV_DIM
IN_CKV = IN_QA + MLA_Q_RANK
IN_KR = IN_CKV + MLA_KV_RANK
IN_DT = IN_KR + LANES
IN_N = IN_DT + LANES
DT_COPIES = 3


def _rms(xf):
    return xf * lax.rsqrt(jnp.mean(xf * xf, axis=-1, keepdims=True) + NORM_EPS)


def _silu(v):
    return v * jax.nn.sigmoid(v)


def _dot(a, b):
    return jnp.dot(a, b, preferred_element_type=F32)


def _dot_nt(a, b):
    return lax.dot_general(a, b, (((1,), (1,)), ((), ())), preferred_element_type=F32)


def _const_spec(shape):
    nd = len(shape)
    return pl.BlockSpec(shape, lambda *_: (0,) * nd, pipeline_mode=pl.Buffered(1))


def _params(sem, vmem=VMEM_LIMIT):
    return pltpu.CompilerParams(dimension_semantics=sem, vmem_limit_bytes=vmem)


def _rope_table_kernel(pos_ref, freq_ref, cos_ref, sin_ref):
    ang = pos_ref[...].astype(F32) * freq_ref[...]
    cos_ref[...] = jnp.cos(ang)
    sin_ref[...] = jnp.sin(ang)


def _rope_tables(positions):
    t = positions.size
    half = MLA_ROPE // 2
    inv_freq = ROPE_THETA ** (-jnp.arange(0, MLA_ROPE, 2, dtype=F32) / MLA_ROPE)
    rows = t * half // LANES
    pos_rep = jnp.repeat(positions.reshape(t), half).reshape(rows, LANES)
    freq = jnp.tile(inv_freq, LANES // half).reshape(1, LANES)
    cos, sin = pl.pallas_call(
        _rope_table_kernel,
        out_shape=(jax.ShapeDtypeStruct((rows, LANES), F32),) * 2,
        name="rope_table",
    )(pos_rep, freq)
    cos = cos.reshape(t, half)
    sin = sin.reshape(t, half)
    zeros = jnp.zeros((t, LANES - MLA_ROPE), F32)
    c2 = jnp.concatenate([cos, cos, zeros], axis=-1)
    s2 = jnp.concatenate([-sin, sin, zeros], axis=-1)
    return c2, s2


def _rope(r, c2, s2):
    half = MLA_ROPE // 2
    lane = lax.broadcasted_iota(jnp.int32, r.shape, 1)
    swapped = jnp.where(lane < half, pltpu.roll(r, LANES - half, 1), pltpu.roll(r, half, 1))
    return r * c2 + swapped * s2


def _in_proj_kernel(x_ref, nw_ref, w_ref, z_ref, xbc_ref, qa_ref, ckv_ref, kr_ref, dt_ref):
    sub = x_ref.shape[0] // ROW_SPLIT
    for r in range(ROW_SPLIT):
        rows = slice(r * sub, (r + 1) * sub)
        h = (_rms(x_ref[rows, :]) * nw_ref[...]).astype(BF16)

        def mm(lo, hi):
            return _dot(h, w_ref[:, lo:hi])

        z_ref[rows, :] = mm(IN_Z, IN_XBC).astype(BF16)
        xbc_ref[rows, :] = mm(IN_XBC, IN_QA).astype(BF16)
        qa_ref[rows, :] = mm(IN_QA, IN_CKV).astype(BF16)
        ckv_ref[rows, :] = mm(IN_CKV, IN_KR).astype(BF16)
        kr_ref[rows, :] = mm(IN_KR, IN_DT)
        dt_ref[rows, :] = mm(IN_DT, IN_N)


def _in_proj(x2d, norm_w, w_in_r, tm):
    t = x2d.shape[0]
    widths = (D_SSM, SSD_CONV_DIM, MLA_Q_RANK, MLA_KV_RANK, LANES, LANES)
    dtypes = (BF16, BF16, BF16, BF16, F32, F32)
    return pl.pallas_call(
        _in_proj_kernel,
        grid=(t // tm,),
        in_specs=[
            pl.BlockSpec((tm, D_MODEL), lambda i: (i, 0)),
            _const_spec((1, D_MODEL)),
            _const_spec((D_MODEL, IN_N)),
        ],
        out_specs=[pl.BlockSpec((tm, w), lambda i: (i, 0)) for w in widths],
        out_shape=[jax.ShapeDtypeStruct((t, w), d) for w, d in zip(widths, dtypes)],
        compiler_params=_params(("parallel",)),
        name="in_proj",
    )(x2d, norm_w, w_in_r)


def _ssd_kernel(z_ref, xbc_ref, dt_ref, cw_ref, cb_ref, dtb_ref, alog_ref, dsk_ref, nw_ref,
                cum_ref, e3_ref, e1_ref, y_ref, xpad_ref, state_ref, yacc_ref, *, ts):
    blk = pl.program_id(1)
    halo = SUBLANES

    @pl.when(blk == 0)
    def _():
        xpad_ref[0:halo, :] = jnp.zeros((halo, SSD_CONV_DIM), F32)
        state_ref[...] = jnp.zeros_like(state_ref)

    xpad_ref[halo:halo + ts, :] = xbc_ref[...].astype(F32)
    conv = cb_ref[...]
    for k in range(SSD_CONV):
        off = halo - (SSD_CONV - 1) + k
        conv = conv + cw_ref[k:k + 1, :] * xpad_ref[off:off + ts, :]
    xpad_ref[0:halo, :] = xpad_ref[ts:ts + halo, :]
    xc = _silu(conv)
    xs = xc[:, :D_SSM]
    bm = xc[:, D_SSM:D_SSM + SSD_GROUPS * SSD_STATE]
    cm = xc[:, D_SSM + SSD_GROUPS * SSD_STATE:]

    lane = lax.broadcasted_iota(jnp.int32, (ts, LANES), 1)
    raw = dt_ref[...] + dtb_ref[...]
    dt = jnp.maximum(raw, 0.0) + jnp.log1p(jnp.exp(-jnp.abs(raw)))
    a = jnp.where(lane < DT_COPIES * SSD_HEADS, -jnp.exp(alog_ref[...]), 0.0)
    adt = dt * a

    def split3(v):
        hi = v.astype(BF16).astype(F32)
        rem = v - hi
        mid = rem.astype(BF16).astype(F32)
        lo = rem - mid
        return jnp.where(lane < SSD_HEADS, hi, jnp.where(lane < 2 * SSD_HEADS, mid, lo)).astype(BF16)

    parts = _dot(cum_ref[...], split3(adt))
    cum = parts
    for shift in (SSD_HEADS, 2 * SSD_HEADS, LANES - SSD_HEADS, LANES - 2 * SSD_HEADS):
        cum = cum + pltpu.roll(parts, shift, 1)
    lane2 = lax.broadcasted_iota(jnp.int32, (2 * ts, LANES), 1)
    cum = jnp.where(lane2 < DT_COPIES * SSD_HEADS, cum, 0.0)
    acs, tot = cum[:ts], cum[ts:]
    eacs = jnp.exp(acs)
    dtdec = dt * jnp.exp(tot - acs)

    stacked = jnp.concatenate([dt, eacs, dtdec], axis=0).astype(BF16)
    expanded = _dot(stacked, e1_ref[...])
    dt_e, eacs_e, dtdec_e = expanded[:ts], expanded[ts:2 * ts], expanded[2 * ts:]

    col_part = _dot(split3(acs), e3_ref[...])

    xdt = (xs * dt_e).astype(BF16)
    xdec = (xs * dtdec_e).astype(BF16)
    bm16 = bm.astype(BF16)
    cm16 = cm.astype(BF16)

    li = lax.broadcasted_iota(jnp.int32, (CHUNK, D_SSM), 0)
    si = lax.broadcasted_iota(jnp.int32, (CHUNK, D_SSM), 1) % CHUNK
    heads_per_mm = 2 * LANES // SSD_HEADDIM
    bw = heads_per_mm * SSD_HEADDIM
    bd_r = lax.broadcasted_iota(jnp.int32, (bw, bw), 0) // CHUNK
    bd_c = lax.broadcasted_iota(jnp.int32, (bw, bw), 1) // SSD_HEADDIM
    bd_mask = bd_r == bd_c

    for c in range(ts // CHUNK):
        r0 = c * CHUNK
        rows = slice(r0, r0 + CHUNK)
        cp = col_part[rows]
        row_part = jnp.sum(jnp.where(li == si, cp, 0.0), axis=0, keepdims=True)
        lmat = jnp.exp(jnp.where(li >= si, cp - row_part, -jnp.inf))

        pieces = []
        for g in range(SSD_GROUPS):
            gl = slice(g * SSD_STATE, (g + 1) * SSD_STATE)
            gw = slice(g * GROUP_W, (g + 1) * GROUP_W)
            cg = cm16[rows, gl]
            bg = bm16[rows, gl]
            heads_g = GROUP_W // CHUNK
            cb_t = _dot_nt(cg, jnp.concatenate([bg] * heads_g, axis=0))
            m_g = (cb_t * lmat[:, gw]).astype(BF16)
            y_off = _dot(cg, state_ref[g].astype(BF16)) * eacs_e[rows, gw]
            y_diag = []
            for j in range(GROUP_W // bw):
                cols = slice(g * GROUP_W + j * bw, g * GROUP_W + (j + 1) * bw)
                xj = xdt[rows, cols]
                bd = jnp.where(bd_mask, jnp.concatenate([xj] * heads_per_mm, axis=0), jnp.zeros((), BF16))
                y_diag.append(_dot(m_g[:, j * bw:(j + 1) * bw], bd))
            pieces.append(jnp.concatenate(y_diag, axis=1) + y_off)
            bg_t = bm[rows, gl].T.astype(BF16)
            new_state = _dot(bg_t, xdec[rows, gw])
            decay = eacs_e[r0 + CHUNK - 1:r0 + CHUNK, gw]
            state_ref[g] = state_ref[g] * decay + new_state
        yacc_ref[rows, :] = jnp.concatenate(pieces, axis=1)

    y = yacc_ref[...] + dsk_ref[...] * xs
    y = y * _silu(z_ref[...].astype(F32))
    outs = []
    for g in range(SSD_GROUPS):
        outs.append(_rms(y[:, g * GROUP_W:(g + 1) * GROUP_W]))
    y_ref[...] = (jnp.concatenate(outs, axis=1) * nw_ref[...]).astype(BF16)


def _ssd(z, xbc, dt, conv_w, conv_b, dtb, alog, dsk, norm_w, batch, seq, ts):
    nblk = seq // ts
    r = jnp.arange(ts)
    same = (r[:, None] // CHUNK) == (r[None, :] // CHUNK)
    cum_mat = jnp.concatenate([same & (r[:, None] >= r[None, :]), same], axis=0).astype(BF16)
    lane_head = jnp.arange(D_SSM) // SSD_HEADDIM
    e_rows = (jnp.arange(LANES)[:, None] % SSD_HEADS) == lane_head[None, :]
    e1 = (e_rows & (jnp.arange(LANES)[:, None] < SSD_HEADS)).astype(BF16)
    e3 = (e_rows & (jnp.arange(LANES)[:, None] < DT_COPIES * SSD_HEADS)).astype(BF16)

    def row_spec(w):
        return pl.BlockSpec((ts, w), lambda b, j: (b * nblk + j, 0))

    return pl.pallas_call(
        functools.partial(_ssd_kernel, ts=ts),
        grid=(batch, nblk),
        in_specs=[
            row_spec(D_SSM), row_spec(SSD_CONV_DIM), row_spec(LANES),
            _const_spec((SSD_CONV, SSD_CONV_DIM)), _const_spec((1, SSD_CONV_DIM)),
            _const_spec((1, LANES)), _const_spec((1, LANES)),
            _const_spec((1, D_SSM)), _const_spec((1, D_SSM)),
            _const_spec((2 * ts, ts)), _const_spec((LANES, D_SSM)), _const_spec((LANES, D_SSM)),
        ],
        out_specs=row_spec(D_SSM),
        out_shape=jax.ShapeDtypeStruct((batch * seq, D_SSM), BF16),
        scratch_shapes=[
            pltpu.VMEM((ts + SUBLANES, SSD_CONV_DIM), F32),
            pltpu.VMEM((SSD_GROUPS, SSD_STATE, GROUP_W), F32),
            pltpu.VMEM((ts, D_SSM), F32),
        ],
        compiler_params=_params(("parallel", "arbitrary")),
        name="ssd",
    )(z, xbc, dt, conv_w, conv_b, dtb, alog, dsk, norm_w, cum_mat, e3, e1)


def _mla_proj_kernel(qa_ref, ckv_ref, kr_ref, c2_ref, s2_ref, qnw_ref, wq_ref, kvnw_ref, wkv_ref,
                     q_ref, k_ref, v_ref):
    c2 = c2_ref[...]
    s2 = s2_ref[...]
    scale = 1.0 / math.sqrt(MLA_NOPE + MLA_ROPE)
    qn = (_rms(qa_ref[...].astype(F32)) * qnw_ref[...]).astype(BF16)
    q = _dot(qn, wq_ref[...]) * scale
    cn = (_rms(ckv_ref[...].astype(F32)) * kvnw_ref[...]).astype(BF16)
    kv = _dot(cn, wkv_ref[...])
    k_rope = _rope(kr_ref[...], c2, s2).astype(BF16)
    for h in range(MLA_HEADS):
        base = h * QK_W
        q_ref[:, base:base + LANES] = q[:, base:base + LANES].astype(BF16)
        q_ref[:, base + LANES:base + QK_W] = _rope(q[:, base + LANES:base + QK_W], c2, s2).astype(BF16)
        k_ref[:, base:base + LANES] = kv[:, h * MLA_NOPE:(h + 1) * MLA_NOPE].astype(BF16)
        k_ref[:, base + LANES:base + QK_W] = k_rope
    v_ref[...] = kv[:, MLA_HEADS * MLA_NOPE:].astype(BF16)


def _mla_proj(qa, ckv, kr, c2, s2, qnw, wq_r, kvnw, wkv_r, tm):
    t = qa.shape[0]

    def row_spec(w):
        return pl.BlockSpec((tm, w), lambda i: (i, 0))

    widths = (MLA_HEADS * QK_W, MLA_HEADS * QK_W, MLA_HEADS * MLA_V)
    return pl.pallas_call(
        _mla_proj_kernel,
        grid=(t // tm,),
        in_specs=[
            row_spec(MLA_Q_RANK), row_spec(MLA_KV_RANK), row_spec(LANES), row_spec(LANES), row_spec(LANES),
            _const_spec((1, MLA_Q_RANK)), _const_spec((MLA_Q_RANK, MLA_HEADS * QK_W)),
            _const_spec((1, MLA_KV_RANK)), _const_spec((MLA_KV_RANK, MLA_HEADS * (MLA_NOPE + MLA_V))),
        ],
        out_specs=[row_spec(w) for w in widths],
        out_shape=[jax.ShapeDtypeStruct((t, w), BF16) for w in widths],
        compiler_params=_params(("parallel",)),
        name="mla_proj",
    )(qa, ckv, kr, c2, s2, qnw, wq_r, kvnw, wkv_r)


def _attention_kernel(q_ref, k_ref, v_ref, o_ref, *, seq, tq):
    row_chunk = lax.broadcasted_iota(jnp.int32, (tq, tq), 0) // CHUNK
    col_chunk = lax.broadcasted_iota(jnp.int32, (tq, tq), 1) // CHUNK
    diag_mask = col_chunk <= row_chunk
    for i in range(seq // tq):
        lo = i * tq
        qi = q_ref[lo:lo + tq, :]
        s_diag = jnp.where(diag_mask, _dot_nt(qi, k_ref[lo:lo + tq, :]), -jnp.inf)
        m = jnp.max(s_diag, axis=-1, keepdims=True)
        if i > 0:
            s_off = _dot_nt(qi, k_ref[0:lo, :])
            m = jnp.maximum(m, jnp.max(s_off, axis=-1, keepdims=True))
        p_diag = jnp.exp(s_diag - m)
        denom = jnp.sum(p_diag, axis=-1, keepdims=True)
        acc = _dot(p_diag.astype(BF16), v_ref[lo:lo + tq, :])
        if i > 0:
            p_off = jnp.exp(s_off - m)
            denom = denom + jnp.sum(p_off, axis=-1, keepdims=True)
            acc = acc + _dot(p_off.astype(BF16), v_ref[0:lo, :])
        o_ref[lo:lo + tq, :] = (acc / denom).astype(BF16)


def _attention(q, k, v, batch, seq, tq):
    q3 = q.reshape(batch, seq, MLA_HEADS * QK_W)
    k3 = k.reshape(batch, seq, MLA_HEADS * QK_W)
    v3 = v.reshape(batch, seq, MLA_HEADS * MLA_V)

    def spec(w):
        return pl.BlockSpec((None, seq, w), lambda b, h: (b, 0, h))

    out = pl.pallas_call(
        functools.partial(_attention_kernel, seq=seq, tq=tq),
        grid=(batch, MLA_HEADS),
        in_specs=[spec(QK_W), spec(QK_W), spec(MLA_V)],
        out_specs=spec(MLA_V),
        out_shape=jax.ShapeDtypeStruct((batch, seq, MLA_HEADS * MLA_V), BF16),
        compiler_params=_params(("parallel", "parallel")),
        name="attention",
    )(q3, k3, v3)
    return out.reshape(batch * seq, MLA_HEADS * MLA_V)


def _out_proj_kernel(x_ref, ys_ref, ym_ref, ws_ref, wm_ref, o_ref):
    o_ref[...] = x_ref[...] + _dot(ys_ref[...], ws_ref[...]) + _dot(ym_ref[...], wm_ref[...])


def _out_proj(x2d, y_ssd, y_mla, w_s, w_m, tm):
    t = x2d.shape[0]
    return pl.pallas_call(
        _out_proj_kernel,
        grid=(t // tm,),
        in_specs=[
            pl.BlockSpec((tm, D_MODEL), lambda i: (i, 0)),
            pl.BlockSpec((tm, D_SSM), lambda i: (i, 0)),
            pl.BlockSpec((tm, MLA_HEADS * MLA_V), lambda i: (i, 0)),
            _const_spec((D_SSM, D_MODEL)),
            _const_spec((MLA_HEADS * MLA_V, D_MODEL)),
        ],
        out_specs=pl.BlockSpec((tm, D_MODEL), lambda i: (i, 0)),
        out_shape=jax.ShapeDtypeStruct((t, D_MODEL), F32),
        compiler_params=_params(("parallel",)),
        name="out_proj",
    )(x2d, y_ssd, y_mla, w_s, w_m)


def _conv_ffn_kernel(x_ref, nw_ref, wg_ref, wu_ref, cwg_ref, cwu_ref, cbg_ref, cbu_ref, wd_ref, o_ref,
                     h_ref, acc_ref, gpad_ref, upad_ref, act_ref, ghalo_ref, uhalo_ref, *, tm, tiles_per_seq):
    i = pl.program_id(0)
    f = pl.program_id(1)
    nf = pl.num_programs(1)
    halo = SUBLANES
    tf = act_ref.shape[1]

    @pl.when(f == 0)
    def _():
        x = x_ref[...]
        h_ref[...] = (_rms(x) * nw_ref[...]).astype(BF16)
        acc_ref[...] = x

    h = h_ref[...]
    seq_start = (i % tiles_per_seq) == 0
    for pad_ref, halo_ref, w_ref in ((gpad_ref, ghalo_ref, wg_ref), (upad_ref, uhalo_ref, wu_ref)):
        prev = halo_ref[f]
        pad_ref[0:halo, :] = jnp.where(seq_start, jnp.zeros_like(prev), prev)
        pad_ref[halo:halo + tm, :] = _dot(h, w_ref[...])
        halo_ref[f] = pad_ref[tm:tm + halo, :]

    def gate_strip(r0, c0):
        cols = slice(c0, c0 + LANES)

        def conv(pad_ref, cw_ref, cb_ref):
            out = cb_ref[:, cols]
            for j in range(FFN_CONV):
                off = r0 + halo - (FFN_CONV - 1) + j
                out = out + cw_ref[j:j + 1, cols] * pad_ref[off:off + FFN_STRIP_ROWS, cols]
            return out

        gate = conv(gpad_ref, cwg_ref, cbg_ref)
        up = conv(upad_ref, cwu_ref, cbu_ref)
        act_ref[r0:r0 + FFN_STRIP_ROWS, cols] = (_silu(gate) * up).astype(BF16)

    rc = tm // FFN_ROW_CHUNKS
    for c in range(FFN_ROW_CHUNKS):
        for r0 in range(c * rc, (c + 1) * rc, FFN_STRIP_ROWS):
            for c0 in range(0, tf, LANES):
                gate_strip(r0, c0)
        rows = slice(c * rc, (c + 1) * rc)
        acc_ref[rows, :] += _dot(act_ref[rows, :], wd_ref[...])

    @pl.when(f == nf - 1)
    def _():
        o_ref[...] = acc_ref[...]


def _conv_ffn(x2d, norm_w, w_up, conv_w, conv_b, w_down, seq, tm, tf):
    t = x2d.shape[0]
    nf = D_FF // tf
    pad = pltpu.VMEM((tm + SUBLANES, tf), F32)
    carry = pltpu.VMEM((nf, SUBLANES, tf), F32)
    return pl.pallas_call(
        functools.partial(_conv_ffn_kernel, tm=tm, tiles_per_seq=seq // tm),
        grid=(t // tm, nf),
        in_specs=[
            pl.BlockSpec((tm, D_MODEL), lambda i, f: (i, 0)),
            _const_spec((1, D_MODEL)),
            pl.BlockSpec((D_MODEL, tf), lambda i, f: (0, f)),
            pl.BlockSpec((D_MODEL, tf), lambda i, f: (0, f + nf)),
            pl.BlockSpec((FFN_CONV, tf), lambda i, f: (0, f)),
            pl.BlockSpec((FFN_CONV, tf), lambda i, f: (0, f + nf)),
            pl.BlockSpec((1, tf), lambda i, f: (0, f)),
            pl.BlockSpec((1, tf), lambda i, f: (0, f + nf)),
            pl.BlockSpec((tf, D_MODEL), lambda i, f: (f, 0)),
        ],
        out_specs=pl.BlockSpec((tm, D_MODEL), lambda i, f: (i, 0)),
        out_shape=jax.ShapeDtypeStruct((t, D_MODEL), F32),
        scratch_shapes=[
            pltpu.VMEM((tm, D_MODEL), BF16),
            pltpu.VMEM((tm, D_MODEL), F32),
            pad, pad,
            pltpu.VMEM((tm, tf), BF16),
            carry, carry,
        ],
        compiler_params=_params(("arbitrary", "arbitrary")),
        name="conv_ffn",
    )(x2d, norm_w, w_up, w_up, conv_w, conv_w, conv_b, conv_b, w_down)


def _ple_final_kernel(x_ref, p_ref, nw_ref, wg_ref, bg_ref, wp_ref, pnw_ref, fnw_ref, o_ref):
    sub = x_ref.shape[0] // ROW_SPLIT
    for r in range(ROW_SPLIT):
        rows = slice(r * sub, (r + 1) * sub)
        x = x_ref[rows, :]
        hn = (_rms(x) * nw_ref[...]).astype(BF16)
        gate = jax.nn.sigmoid(_dot(hn, wg_ref[...]) + bg_ref[...])
        proj = _dot(p_ref[rows, :].astype(BF16), wp_ref[...])
        x = x + gate * (_rms(proj) * pnw_ref[...])
        o_ref[rows, :] = _rms(x) * fnw_ref[...]


def _ple_final(x2d, p2d, norm_w, w_gate, b_gate, w_proj, post_w, final_w, tm):
    t = x2d.shape[0]
    return pl.pallas_call(
        _ple_final_kernel,
        grid=(t // tm,),
        in_specs=[
            pl.BlockSpec((tm, D_MODEL), lambda i: (i, 0)),
            pl.BlockSpec((tm, PLE_DIM), lambda i: (i, 0)),
            _const_spec((1, D_MODEL)),
            _const_spec((D_MODEL, D_MODEL)),
            _const_spec((1, D_MODEL)),
            _const_spec((PLE_DIM, D_MODEL)),
            _const_spec((1, D_MODEL)),
            _const_spec((1, D_MODEL)),
        ],
        out_specs=pl.BlockSpec((tm, D_MODEL), lambda i: (i, 0)),
        out_shape=jax.ShapeDtypeStruct((t, D_MODEL), F32),
        compiler_params=_params(("parallel",)),
        name="ple_final",
    )(x2d, p2d, norm_w, w_gate, b_gate, w_proj, post_w, final_w)


def _pad_cols(w, width):
    return jnp.pad(w, ((0, 0), (0, width - w.shape[1])))


def _prep_w_in(w_in):
    o_xbc = D_SSM
    o_dt = o_xbc + SSD_CONV_DIM
    o_qa = o_dt + SSD_HEADS
    o_kv = o_qa + MLA_Q_RANK
    o_kr = o_kv + MLA_KV_RANK
    w_dt = w_in[:, o_dt:o_qa]
    return jnp.concatenate([
        w_in[:, :o_dt],
        w_in[:, o_qa:o_kr],
        _pad_cols(w_in[:, o_kr:], LANES),
        _pad_cols(jnp.concatenate([w_dt] * DT_COPIES, axis=1), LANES),
    ], axis=1).astype(BF16)


def _prep_w_q(w_q_b):
    w = w_q_b.reshape(MLA_Q_RANK, MLA_HEADS, MLA_NOPE + MLA_ROPE)
    w = jnp.pad(w, ((0, 0), (0, 0), (0, QK_W - MLA_NOPE - MLA_ROPE)))
    return w.reshape(MLA_Q_RANK, MLA_HEADS * QK_W).astype(BF16)


def _prep_w_kv(w_kv_b):
    w = w_kv_b.reshape(MLA_KV_RANK, MLA_HEADS, MLA_NOPE + MLA_V)
    w_k = w[:, :, :MLA_NOPE].reshape(MLA_KV_RANK, MLA_HEADS * MLA_NOPE)
    w_v = w[:, :, MLA_NOPE:].reshape(MLA_KV_RANK, MLA_HEADS * MLA_V)
    return jnp.concatenate([w_k, w_v], axis=1).astype(BF16)


def _lane_row(v, width=None):
    v = v.reshape(1, -1).astype(F32)
    return v if width is None else _pad_cols(v, width)


def kernel(x, p, positions, mix_norm_w, w_in, conv_w, conv_b, dt_bias, a_log, d_skip, ssd_norm_w,
           q_a_norm_w, w_q_b, kv_a_norm_w, w_kv_b, w_out, ffn_norm_w, w_ffn_up, ffn_conv_w,
           ffn_conv_b, w_ffn_down, ple_norm_w, w_ple_gate, b_ple_gate, w_ple_proj, ple_post_norm_w,
           final_norm_w):
    batch, seq, _ = x.shape
    depth = w_in.shape[0]
    t = batch * seq
    x2d = x.reshape(t, D_MODEL)
    c2, s2 = _rope_tables(positions)

    assert depth == 1, "the final RMSNorm is fused into the single layer's last kernel"
    i = 0
    z, xbc, qa, ckv, kr, dt = _in_proj(x2d, _lane_row(mix_norm_w[i]), _prep_w_in(w_in[i]), tm=512)
    y_ssd = _ssd(
        z, xbc, dt, conv_w[i], _lane_row(conv_b[i]),
        _lane_row(jnp.tile(dt_bias[i], DT_COPIES), LANES),
        _lane_row(jnp.tile(a_log[i], DT_COPIES), LANES),
        _lane_row(jnp.repeat(d_skip[i], SSD_HEADDIM)), _lane_row(ssd_norm_w[i]),
        batch, seq, ts=256)
    q, k, v = _mla_proj(qa, ckv, kr, c2, s2, _lane_row(q_a_norm_w[i]), _prep_w_q(w_q_b[i]),
                        _lane_row(kv_a_norm_w[i]), _prep_w_kv(w_kv_b[i]), tm=512)
    y_mla = _attention(q, k, v, batch, seq, tq=256)
    w_o = w_out[i].astype(BF16)
    x2d = _out_proj(x2d, y_ssd, y_mla, w_o[:D_SSM], w_o[D_SSM:], tm=512)
    x2d = _conv_ffn(x2d, _lane_row(ffn_norm_w[i]), w_ffn_up[i].astype(BF16), ffn_conv_w[i],
                    _lane_row(ffn_conv_b[i]), w_ffn_down[i].astype(BF16), seq, tm=512, tf=512)
    x2d = _ple_final(x2d, p[i].reshape(t, PLE_DIM), _lane_row(ple_norm_w[i]),
                     w_ple_gate[i].astype(BF16), _lane_row(b_ple_gate[i]), w_ple_proj[i].astype(BF16),
                     _lane_row(ple_post_norm_w[i]), _lane_row(final_norm_w), tm=512)
    return x2d.reshape(batch, seq, D_MODEL)
```

```python
import functools
import math

import jax
import jax.numpy as jnp
from jax import lax
from jax.experimental import pallas as pl
from jax.experimental.pallas import tpu as pltpu

F32 = jnp.float32
BF16 = jnp.bfloat16

D_MODEL = 2048
CHUNK = 64
D_SSM = 1024
SSD_HEADDIM = 64
SSD_HEADS = 16
SSD_GROUPS = 2
SSD_STATE = 128
SSD_CONV = 4
SSD_CONV_DIM = D_SSM + 2 * SSD_GROUPS * SSD_STATE
GROUP_W = D_SSM // SSD_GROUPS
MLA_V = 128
MLA_HEADS = 8
MLA_NOPE = 128
MLA_ROPE = 64
MLA_Q_RANK = 512
MLA_KV_RANK = 256
ROPE_THETA = 10000.0
D_FF = 5632
FFN_CONV = 3
PLE_DIM = 256
NORM_EPS = 1e-6

LANES = 128
SUBLANES = 8
QK_W = 2 * LANES
VMEM_LIMIT = 56 * 1024 * 1024
ROW_SPLIT = 2
FFN_COL_CHUNK = 2 * LANES

IN_Z = 0
IN_XBC = IN_Z + D_SSM
IN_QA = IN_XBC + SSD_CONV_DIM
IN_CKV = IN_QA + MLA_Q_RANK
IN_KR = IN_CKV + MLA_KV_RANK
IN_DT = IN_KR + LANES
IN_N = IN_DT + LANES
DT_COPIES = 3


def _rms(xf):
    return xf * lax.rsqrt(jnp.mean(xf * xf, axis=-1, keepdims=True) + NORM_EPS)


def _silu(v):
    return v * jax.nn.sigmoid(v)


def _dot(a, b):
    return jnp.dot(a, b, preferred_element_type=F32)


def _dot_nt(a, b):
    return lax.dot_general(a, b, (((1,), (1,)), ((), ())), preferred_element_type=F32)


def _const_spec(shape):
    nd = len(shape)
    return pl.BlockSpec(shape, lambda *_: (0,) * nd, pipeline_mode=pl.Buffered(1))


def _params(sem, vmem=VMEM_LIMIT):
    return pltpu.CompilerParams(dimension_semantics=sem, vmem_limit_bytes=vmem)


def _rope_table_kernel(pos_ref, freq_ref, cos_ref, sin_ref):
    ang = pos_ref[...].astype(F32) * freq_ref[...]
    cos_ref[...] = jnp.cos(ang)
    sin_ref[...] = jnp.sin(ang)


def _rope_tables(positions):
    t = positions.size
    half = MLA_ROPE // 2
    inv_freq = ROPE_THETA ** (-jnp.arange(0, MLA_ROPE, 2, dtype=F32) / MLA_ROPE)
    rows = t * half // LANES
    pos_rep = jnp.repeat(positions.reshape(t), half).reshape(rows, LANES)
    freq = jnp.tile(inv_freq, LANES // half).reshape(1, LANES)
    cos, sin = pl.pallas_call(
        _rope_table_kernel,
        out_shape=(jax.ShapeDtypeStruct((rows, LANES), F32),) * 2,
        name="rope_table",
    )(pos_rep, freq)
    cos = cos.reshape(t, half)
    sin = sin.reshape(t, half)
    zeros = jnp.zeros((t, LANES - MLA_ROPE), F32)
    c2 = jnp.concatenate([cos, cos, zeros], axis=-1)
    s2 = jnp.concatenate([-sin, sin, zeros], axis=-1)
    return c2, s2


def _rope(r, c2, s2):
    half = MLA_ROPE // 2
    lane = lax.broadcasted_iota(jnp.int32, r.shape, 1)
    swapped = jnp.where(lane < half, pltpu.roll(r, LANES - half, 1), pltpu.roll(r, half, 1))
    return r * c2 + swapped * s2


def _in_proj_kernel(x_ref, nw_ref, w_ref, z_ref, xbc_ref, qa_ref, ckv_ref, kr_ref, dt_ref):
    sub = x_ref.shape[0] // ROW_SPLIT
    for r in range(ROW_SPLIT):
        rows = slice(r * sub, (r + 1) * sub)
        h = (_rms(x_ref[rows, :]) * nw_ref[...]).astype(BF16)

        def mm(lo, hi):
            return _dot(h, w_ref[:, lo:hi])

        z_ref[rows, :] = mm(IN_Z, IN_XBC).astype(BF16)
        xbc_ref[rows, :] = mm(IN_XBC, IN_QA).astype(BF16)
        qa_ref[rows, :] = mm(IN_QA, IN_CKV).astype(BF16)
        ckv_ref[rows, :] = mm(IN_CKV, IN_KR).astype(BF16)
        kr_ref[rows, :] = mm(IN_KR, IN_DT)
        dt_ref[rows, :] = mm(IN_DT, IN_N)


def _in_proj(x2d, norm_w, w_in_r, tm):
    t = x2d.shape[0]
    widths = (D_SSM, SSD_CONV_DIM, MLA_Q_RANK, MLA_KV_RANK, LANES, LANES)
    dtypes = (BF16, BF16, BF16, BF16, F32, F32)
    return pl.pallas_call(
        _in_proj_kernel,
        grid=(t // tm,),
        in_specs=[
            pl.BlockSpec((tm, D_MODEL), lambda i: (i, 0)),
            _const_spec((1, D_MODEL)),
            _const_spec((D_MODEL, IN_N)),
        ],
        out_specs=[pl.BlockSpec((tm, w), lambda i: (i, 0)) for w in widths],
        out_shape=[jax.ShapeDtypeStruct((t, w), d) for w, d in zip(widths, dtypes)],
        compiler_params=_params(("parallel",)),
        name="in_proj",
    )(x2d, norm_w, w_in_r)


def _ssd_kernel(z_ref, xbc_ref, dt_ref, cw_ref, cb_ref, dtb_ref, alog_ref, dsk_ref, nw_ref,
                cum_ref, e3_ref, e1_ref, y_ref, xpad_ref, state_ref, yacc_ref, *, ts):
    blk = pl.program_id(1)
    halo = SUBLANES

    @pl.when(blk == 0)
    def _():
        xpad_ref[0:halo, :] = jnp.zeros((halo, SSD_CONV_DIM), F32)
        state_ref[...] = jnp.zeros_like(state_ref)

    xpad_ref[halo:halo + ts, :] = xbc_ref[...].astype(F32)
    conv = cb_ref[...]
    for k in range(SSD_CONV):
        off = halo - (SSD_CONV - 1) + k
        conv = conv + cw_ref[k:k + 1, :] * xpad_ref[off:off + ts, :]
    xpad_ref[0:halo, :] = xpad_ref[ts:ts + halo, :]
    xc = _silu(conv)
    xs = xc[:, :D_SSM]
    bm = xc[:, D_SSM:D_SSM + SSD_GROUPS * SSD_STATE]
    cm = xc[:, D_SSM + SSD_GROUPS * SSD_STATE:]

    lane = lax.broadcasted_iota(jnp.int32, (ts, LANES), 1)
    raw = dt_ref[...] + dtb_ref[...]
    dt = jnp.maximum(raw, 0.0) + jnp.log1p(jnp.exp(-jnp.abs(raw)))
    a = jnp.where(lane < DT_COPIES * SSD_HEADS, -jnp.exp(alog_ref[...]), 0.0)
    adt = dt * a

    def split3(v):
        hi = v.astype(BF16).astype(F32)
        rem = v - hi
        mid = rem.astype(BF16).astype(F32)
        lo = rem - mid
        return jnp.where(lane < SSD_HEADS, hi, jnp.where(lane < 2 * SSD_HEADS, mid, lo)).astype(BF16)

    parts = _dot(cum_ref[...], split3(adt))
    cum = parts
    for shift in (SSD_HEADS, 2 * SSD_HEADS, LANES - SSD_HEADS, LANES - 2 * SSD_HEADS):
        cum = cum + pltpu.roll(parts, shift, 1)
    lane2 = lax.broadcasted_iota(jnp.int32, (2 * ts, LANES), 1)
    cum = jnp.where(lane2 < DT_COPIES * SSD_HEADS, cum, 0.0)
    acs, tot = cum[:ts], cum[ts:]
    eacs = jnp.exp(acs)
    dtdec = dt * jnp.exp(tot - acs)

    stacked = jnp.concatenate([dt, eacs, dtdec], axis=0).astype(BF16)
    expanded = _dot(stacked, e1_ref[...])
    dt_e, eacs_e, dtdec_e = expanded[:ts], expanded[ts:2 * ts], expanded[2 * ts:]

    col_part = _dot(split3(acs), e3_ref[...])

    xdt = (xs * dt_e).astype(BF16)
    xdec = (xs * dtdec_e).astype(BF16)
    bm16 = bm.astype(BF16)
    cm16 = cm.astype(BF16)

    li = lax.broadcasted_iota(jnp.int32, (CHUNK, D_SSM), 0)
    si = lax.broadcasted_iota(jnp.int32, (CHUNK, D_SSM), 1) % CHUNK
    heads_per_mm = 2 * LANES // SSD_HEADDIM
    bw = heads_per_mm * SSD_HEADDIM
    bd_r = lax.broadcasted_iota(jnp.int32, (bw, bw), 0) // CHUNK
    bd_c = lax.broadcasted_iota(jnp.int32, (bw, bw), 1) // SSD_HEADDIM
    bd_mask = bd_r == bd_c

    for c in range(ts // CHUNK):
        r0 = c * CHUNK
        rows = slice(r0, r0 + CHUNK)
        cp = col_part[rows]
        row_part = jnp.sum(jnp.where(li == si, cp, 0.0), axis=0, keepdims=True)
        lmat = jnp.exp(jnp.where(li >= si, cp - row_part, -jnp.inf))

        pieces = []
        for g in range(SSD_GROUPS):
            gl = slice(g * SSD_STATE, (g + 1) * SSD_STATE)
            gw = slice(g * GROUP_W, (g + 1) * GROUP_W)
            cg = cm16[rows, gl]
            bg = bm16[rows, gl]
            heads_g = GROUP_W // CHUNK
            cb_t = _dot_nt(cg, jnp.concatenate([bg] * heads_g, axis=0))
            m_g = (cb_t * lmat[:, gw]).astype(BF16)
            y_off = _dot(cg, state_ref[g].astype(BF16)) * eacs_e[rows, gw]
            y_diag = []
            for j in range(GROUP_W // bw):
                cols = slice(g * GROUP_W + j * bw, g * GROUP_W + (j + 1) * bw)
                xj = xdt[rows, cols]
                bd = jnp.where(bd_mask, jnp.concatenate([xj] * heads_per_mm, axis=0), jnp.zeros((), BF16))
                y_diag.append(_dot(m_g[:, j * bw:(j + 1) * bw], bd))
            pieces.append(jnp.concatenate(y_diag, axis=1) + y_off)
            bg_t = bm[rows, gl].T.astype(BF16)
            new_state = _dot(bg_t, xdec[rows, gw])
            decay = eacs_e[r0 + CHUNK - 1:r0 + CHUNK, gw]
            state_ref[g] = state_ref[g] * decay + new_state
        yacc_ref[rows, :] = jnp.concatenate(pieces, axis=1)

    y = yacc_ref[...] + dsk_ref[...] * xs
    y = y * _silu(z_ref[...].astype(F32))
    outs = []
    for g in range(SSD_GROUPS):
        outs.append(_rms(y[:, g * GROUP_W:(g + 1) * GROUP_W]))
    y_ref[...] = (jnp.concatenate(outs, axis=1) * nw_ref[...]).astype(BF16)


def _ssd(z, xbc, dt, conv_w, conv_b, dtb, alog, dsk, norm_w, batch, seq, ts):
    nblk = seq // ts
    r = jnp.arange(ts)
    same = (r[:, None] // CHUNK) == (r[None, :] // CHUNK)
    cum_mat = jnp.concatenate([same & (r[:, None] >= r[None, :]), same], axis=0).astype(BF16)
    lane_head = jnp.arange(D_SSM) // SSD_HEADDIM
    e_rows = (jnp.arange(LANES)[:, None] % SSD_HEADS) == lane_head[None, :]
    e1 = (e_rows & (jnp.arange(LANES)[:, None] < SSD_HEADS)).astype(BF16)
    e3 = (e_rows & (jnp.arange(LANES)[:, None] < DT_COPIES * SSD_HEADS)).astype(BF16)

    def row_spec(w):
        return pl.BlockSpec((ts, w), lambda b, j: (b * nblk + j, 0))

    return pl.pallas_call(
        functools.partial(_ssd_kernel, ts=ts),
        grid=(batch, nblk),
        in_specs=[
            row_spec(D_SSM), row_spec(SSD_CONV_DIM), row_spec(LANES),
            _const_spec((SSD_CONV, SSD_CONV_DIM)), _const_spec((1, SSD_CONV_DIM)),
            _const_spec((1, LANES)), _const_spec((1, LANES)),
            _const_spec((1, D_SSM)), _const_spec((1, D_SSM)),
            _const_spec((2 * ts, ts)), _const_spec((LANES, D_SSM)), _const_spec((LANES, D_SSM)),
        ],
        out_specs=row_spec(D_SSM),
        out_shape=jax.ShapeDtypeStruct((batch * seq, D_SSM), BF16),
        scratch_shapes=[
            pltpu.VMEM((ts + SUBLANES, SSD_CONV_DIM), F32),
            pltpu.VMEM((SSD_GROUPS, SSD_STATE, GROUP_W), F32),
            pltpu.VMEM((ts, D_SSM), F32),
        ],
        compiler_params=_params(("parallel", "arbitrary")),
        name="ssd",
    )(z, xbc, dt, conv_w, conv_b, dtb, alog, dsk, norm_w, cum_mat, e3, e1)


def _mla_proj_kernel(qa_ref, ckv_ref, kr_ref, c2_ref, s2_ref, qnw_ref, wq_ref, kvnw_ref, wkv_ref,
                     q_ref, k_ref, v_ref):
    c2 = c2_ref[...]
    s2 = s2_ref[...]
    scale = 1.0 / math.sqrt(MLA_NOPE + MLA_ROPE)
    qn = (_rms(qa_ref[...].astype(F32)) * qnw_ref[...]).astype(BF16)
    q = _dot(qn, wq_ref[...]) * scale
    cn = (_rms(ckv_ref[...].astype(F32)) * kvnw_ref[...]).astype(BF16)
    kv = _dot(cn, wkv_ref[...])
    k_rope = _rope(kr_ref[...], c2, s2).astype(BF16)
    for h in range(MLA_HEADS):
        base = h * QK_W
        q_ref[:, base:base + LANES] = q[:, base:base + LANES].astype(BF16)
        q_ref[:, base + LANES:base + QK_W] = _rope(q[:, base + LANES:base + QK_W], c2, s2).astype(BF16)
        k_ref[:, base:base + LANES] = kv[:, h * MLA_NOPE:(h + 1) * MLA_NOPE].astype(BF16)
        k_ref[:, base + LANES:base + QK_W] = k_rope
    v_ref[...] = kv[:, MLA_HEADS * MLA_NOPE:].astype(BF16)


def _mla_proj(qa, ckv, kr, c2, s2, qnw, wq_r, kvnw, wkv_r, tm):
    t = qa.shape[0]

    def row_spec(w):
        return pl.BlockSpec((tm, w), lambda i: (i, 0))

    widths = (MLA_HEADS * QK_W, MLA_HEADS * QK_W, MLA_HEADS * MLA_V)
    return pl.pallas_call(
        _mla_proj_kernel,
        grid=(t // tm,),
        in_specs=[
            row_spec(MLA_Q_RANK), row_spec(MLA_KV_RANK), row_spec(LANES), row_spec(LANES), row_spec(LANES),
            _const_spec((1, MLA_Q_RANK)), _const_spec((MLA_Q_RANK, MLA_HEADS * QK_W)),
            _const_spec((1, MLA_KV_RANK)), _const_spec((MLA_KV_RANK, MLA_HEADS * (MLA_NOPE + MLA_V))),
        ],
        out_specs=[row_spec(w) for w in widths],
        out_shape=[jax.ShapeDtypeStruct((t, w), BF16) for w in widths],
        compiler_params=_params(("parallel",)),
        name="mla_proj",
    )(qa, ckv, kr, c2, s2, qnw, wq_r, kvnw, wkv_r)


def _attention_kernel(q_ref, k_ref, v_ref, o_ref, *, seq, tq):
    row_chunk = lax.broadcasted_iota(jnp.int32, (tq, tq), 0) // CHUNK
    col_chunk = lax.broadcasted_iota(jnp.int32, (tq, tq), 1) // CHUNK
    diag_mask = col_chunk <= row_chunk
    for i in range(seq // tq):
        lo = i * tq
        qi = q_ref[lo:lo + tq, :]
        s_diag = jnp.where(diag_mask, _dot_nt(qi, k_ref[lo:lo + tq, :]), -jnp.inf)
        m = jnp.max(s_diag, axis=-1, keepdims=True)
        if i > 0:
            s_off = _dot_nt(qi, k_ref[0:lo, :])
            m = jnp.maximum(m, jnp.max(s_off, axis=-1, keepdims=True))
        p_diag = jnp.exp(s_diag - m)
        denom = jnp.sum(p_diag, axis=-1, keepdims=True)
        acc = _dot(p_diag.astype(BF16), v_ref[lo:lo + tq, :])
        if i > 0:
            p_off = jnp.exp(s_off - m)
            denom = denom + jnp.sum(p_off, axis=-1, keepdims=True)
            acc = acc + _dot(p_off.astype(BF16), v_ref[0:lo, :])
        o_ref[lo:lo + tq, :] = (acc / denom).astype(BF16)


def _attention(q, k, v, batch, seq, tq):
    q3 = q.reshape(batch, seq, MLA_HEADS * QK_W)
    k3 = k.reshape(batch, seq, MLA_HEADS * QK_W)
    v3 = v.reshape(batch, seq, MLA_HEADS * MLA_V)

    def spec(w):
        return pl.BlockSpec((None, seq, w), lambda b, h: (b, 0, h))

    out = pl.pallas_call(
        functools.partial(_attention_kernel, seq=seq, tq=tq),
        grid=(batch, MLA_HEADS),
        in_specs=[spec(QK_W), spec(QK_W), spec(MLA_V)],
        out_specs=spec(MLA_V),
        out_shape=jax.ShapeDtypeStruct((batch, seq, MLA_HEADS * MLA_V), BF16),
        compiler_params=_params(("parallel", "parallel")),
        name="attention",
    )(q3, k3, v3)
    return out.reshape(batch * seq, MLA_HEADS * MLA_V)


def _out_proj_kernel(x_ref, ys_ref, ym_ref, ws_ref, wm_ref, o_ref):
    o_ref[...] = x_ref[...] + _dot(ys_ref[...], ws_ref[...]) + _dot(ym_ref[...], wm_ref[...])


def _out_proj(x2d, y_ssd, y_mla, w_s, w_m, tm):
    t = x2d.shape[0]
    return pl.pallas_call(
        _out_proj_kernel,
        grid=(t // tm,),
        in_specs=[
            pl.BlockSpec((tm, D_MODEL), lambda i: (i, 0)),
            pl.BlockSpec((tm, D_SSM), lambda i: (i, 0)),
            pl.BlockSpec((tm, MLA_HEADS * MLA_V), lambda i: (i, 0)),
            _const_spec((D_SSM, D_MODEL)),
            _const_spec((MLA_HEADS * MLA_V, D_MODEL)),
        ],
        out_specs=pl.BlockSpec((tm, D_MODEL), lambda i: (i, 0)),
        out_shape=jax.ShapeDtypeStruct((t, D_MODEL), F32),
        compiler_params=_params(("parallel",)),
        name="out_proj",
    )(x2d, y_ssd, y_mla, w_s, w_m)


def _conv_ffn_kernel(x_ref, nw_ref, wg_ref, wu_ref, cwg_ref, cwu_ref, cbg_ref, cbu_ref, wd_ref, o_ref,
                     h_ref, acc_ref, gshift_ref, ushift_ref, ghalo_ref, uhalo_ref, *, tm, tiles_per_seq):
    i = pl.program_id(0)
    f = pl.program_id(1)
    nf = pl.num_programs(1)
    halo = SUBLANES

    @pl.when(f == 0)
    def _():
        x = x_ref[...]
        h_ref[...] = (_rms(x) * nw_ref[...]).astype(BF16)
        acc_ref[...] = x

    @pl.when(jnp.logical_and(i == 0, f == 0))
    def _():
        for shift_ref in (gshift_ref, ushift_ref):
            shift_ref[:, tm:tm + halo, :] = jnp.zeros((FFN_CONV, halo, shift_ref.shape[2]), F32)

    h = h_ref[...]
    seq_start = (i % tiles_per_seq) == 0

    def up_project(w_ref, shift_ref, halo_ref, cols):
        u = _dot(h, w_ref[:, cols])
        shift_ref[0, 0:tm, cols] = u
        for d in range(1, FFN_CONV):
            prev = halo_ref[f, d - 1, :, cols]
            shift_ref[d, 0:halo, cols] = jnp.where(seq_start, jnp.zeros_like(prev), prev)
            shift_ref[d, d:d + tm, cols] = u
            halo_ref[f, d - 1, :, cols] = shift_ref[d, tm:tm + halo, cols]

    def conv(shift_ref, cw_ref, cb_ref, cols):
        out = cb_ref[:, cols]
        for j in range(FFN_CONV):
            out = out + cw_ref[j:j + 1, cols] * shift_ref[FFN_CONV - 1 - j, 0:tm, cols]
        return out

    chunks = [slice(c0, c0 + FFN_COL_CHUNK) for c0 in range(0, gshift_ref.shape[2], FFN_COL_CHUNK)]
    for cols in chunks:
        up_project(wg_ref, gshift_ref, ghalo_ref, cols)
        up_project(wu_ref, ushift_ref, uhalo_ref, cols)
    for cols in chunks:
        gate = _silu(conv(gshift_ref, cwg_ref, cbg_ref, cols))
        act = (gate * conv(ushift_ref, cwu_ref, cbu_ref, cols)).astype(BF16)
        acc_ref[...] += _dot(act, wd_ref[cols, :])

    @pl.when(f == nf - 1)
    def _():
        o_ref[...] = acc_ref[...]


def _conv_ffn(x2d, norm_w, w_up, conv_w, conv_b, w_down, seq, tm, tf):
    t = x2d.shape[0]
    nf = D_FF // tf
    shifted = pltpu.VMEM((FFN_CONV, tm + SUBLANES, tf), F32)
    carry = pltpu.VMEM((nf, FFN_CONV - 1, SUBLANES, tf), F32)
    return pl.pallas_call(
        functools.partial(_conv_ffn_kernel, tm=tm, tiles_per_seq=seq // tm),
        grid=(t // tm, nf),
        in_specs=[
            pl.BlockSpec((tm, D_MODEL), lambda i, f: (i, 0)),
            _const_spec((1, D_MODEL)),
            pl.BlockSpec((D_MODEL, tf), lambda i, f: (0, f)),
            pl.BlockSpec((D_MODEL, tf), lambda i, f: (0, f + nf)),
            pl.BlockSpec((FFN_CONV, tf), lambda i, f: (0, f)),
            pl.BlockSpec((FFN_CONV, tf), lambda i, f: (0, f + nf)),
            pl.BlockSpec((1, tf), lambda i, f: (0, f)),
            pl.BlockSpec((1, tf), lambda i, f: (0, f + nf)),
            pl.BlockSpec((tf, D_MODEL), lambda i, f: (f, 0)),
        ],
        out_specs=pl.BlockSpec((tm, D_MODEL), lambda i, f: (i, 0)),
        out_shape=jax.ShapeDtypeStruct((t, D_MODEL), F32),
        scratch_shapes=[
            pltpu.VMEM((tm, D_MODEL), BF16),
            pltpu.VMEM((tm, D_MODEL), F32),
            shifted, shifted,
            carry, carry,
        ],
        compiler_params=_params(("arbitrary", "arbitrary")),
        name="conv_ffn",
    )(x2d, norm_w, w_up, w_up, conv_w, conv_w, conv_b, conv_b, w_down)


def _ple_final_kernel(x_ref, p_ref, nw_ref, wg_ref, bg_ref, wp_ref, pnw_ref, fnw_ref, o_ref):
    sub = x_ref.shape[0] // ROW_SPLIT
    for r in range(ROW_SPLIT):
        rows = slice(r * sub, (r + 1) * sub)
        x = x_ref[rows, :]
        hn = (_rms(x) * nw_ref[...]).astype(BF16)
        gate = jax.nn.sigmoid(_dot(hn, wg_ref[...]) + bg_ref[...])
        proj = _dot(p_ref[rows, :].astype(BF16), wp_ref[...])
        x = x + gate * (_rms(proj) * pnw_ref[...])
        o_ref[rows, :] = _rms(x) * fnw_ref[...]


def _ple_final(x2d, p2d, norm_w, w_gate, b_gate, w_proj, post_w, final_w, tm):
    t = x2d.shape[0]
    return pl.pallas_call(
        _ple_final_kernel,
        grid=(t // tm,),
        in_specs=[
            pl.BlockSpec((tm, D_MODEL), lambda i: (i, 0)),
            pl.BlockSpec((tm, PLE_DIM), lambda i: (i, 0)),
            _const_spec((1, D_MODEL)),
            _const_spec((D_MODEL, D_MODEL)),
            _const_spec((1, D_MODEL)),
            _const_spec((PLE_DIM, D_MODEL)),
            _const_spec((1, D_MODEL)),
            _const_spec((1, D_MODEL)),
        ],
        out_specs=pl.BlockSpec((tm, D_MODEL), lambda i: (i, 0)),
        out_shape=jax.ShapeDtypeStruct((t, D_MODEL), F32),
        compiler_params=_params(("parallel",)),
        name="ple_final",
    )(x2d, p2d, norm_w, w_gate, b_gate, w_proj, post_w, final_w)


def _pad_cols(w, width):
    return jnp.pad(w, ((0, 0), (0, width - w.shape[1])))


def _prep_w_in(w_in):
    o_xbc = D_SSM
    o_dt = o_xbc + SSD_CONV_DIM
    o_qa = o_dt + SSD_HEADS
    o_kv = o_qa + MLA_Q_RANK
    o_kr = o_kv + MLA_KV_RANK
    w_dt = w_in[:, o_dt:o_qa]
    return jnp.concatenate([
        w_in[:, :o_dt],
        w_in[:, o_qa:o_kr],
        _pad_cols(w_in[:, o_kr:], LANES),
        _pad_cols(jnp.concatenate([w_dt] * DT_COPIES, axis=1), LANES),
    ], axis=1).astype(BF16)


def _prep_w_q(w_q_b):
    w = w_q_b.reshape(MLA_Q_RANK, MLA_HEADS, MLA_NOPE + MLA_ROPE)
    w = jnp.pad(w, ((0, 0), (0, 0), (0, QK_W - MLA_NOPE - MLA_ROPE)))
    return w.reshape(MLA_Q_RANK, MLA_HEADS * QK_W).astype(BF16)


def _prep_w_kv(w_kv_b):
    w = w_kv_b.reshape(MLA_KV_RANK, MLA_HEADS, MLA_NOPE + MLA_V)
    w_k = w[:, :, :MLA_NOPE].reshape(MLA_KV_RANK, MLA_HEADS * MLA_NOPE)
    w_v = w[:, :, MLA_NOPE:].reshape(MLA_KV_RANK, MLA_HEADS * MLA_V)
    return jnp.concatenate([w_k, w_v], axis=1).astype(BF16)


def _lane_row(v, width=None):
    v = v.reshape(1, -1).astype(F32)
    return v if width is None else _pad_cols(v, width)


def kernel(x, p, positions, mix_norm_w, w_in, conv_w, conv_b, dt_bias, a_log, d_skip, ssd_norm_w,
           q_a_norm_w, w_q_b, kv_a_norm_w, w_kv_b, w_out, ffn_norm_w, w_ffn_up, ffn_conv_w,
           ffn_conv_b, w_ffn_down, ple_norm_w, w_ple_gate, b_ple_gate, w_ple_proj, ple_post_norm_w,
           final_norm_w):
    batch, seq, _ = x.shape
    depth = w_in.shape[0]
    t = batch * seq
    x2d = x.reshape(t, D_MODEL)
    c2, s2 = _rope_tables(positions)

    assert depth == 1, "the final RMSNorm is fused into the single layer's last kernel"
    i = 0
    z, xbc, qa, ckv, kr, dt = _in_proj(x2d, _lane_row(mix_norm_w[i]), _prep_w_in(w_in[i]), tm=512)
    y_ssd = _ssd(
        z, xbc, dt, conv_w[i], _lane_row(conv_b[i]),
        _lane_row(jnp.tile(dt_bias[i], DT_COPIES), LANES),
        _lane_row(jnp.tile(a_log[i], DT_COPIES), LANES),
        _lane_row(jnp.repeat(d_skip[i], SSD_HEADDIM)), _lane_row(ssd_norm_w[i]),
        batch, seq, ts=256)
    q, k, v = _mla_proj(qa, ckv, kr, c2, s2, _lane_row(q_a_norm_w[i]), _prep_w_q(w_q_b[i]),
                        _lane_row(kv_a_norm_w[i]), _prep_w_kv(w_kv_b[i]), tm=512)
    y_mla = _attention(q, k, v, batch, seq, tq=256)
    w_o = w_out[i].astype(BF16)
    x2d = _out_proj(x2d, y_ssd, y_mla, w_o[:D_SSM], w_o[D_SSM:], tm=512)
    x2d = _conv_ffn(x2d, _lane_row(ffn_norm_w[i]), w_ffn_up[i].astype(BF16), ffn_conv_w[i],
                    _lane_row(ffn_conv_b[i]), w_ffn_down[i].astype(BF16), seq, tm=512, tf=512)
    x2d = _ple_final(x2d, p[i].reshape(t, PLE_DIM), _lane_row(ple_norm_w[i]),
                     w_ple_gate[i].astype(BF16), _lane_row(b_ple_gate[i]), w_ple_proj[i].astype(BF16),
                     _lane_row(ple_post_norm_w[i]), _lane_row(final_norm_w), tm=512)
    return x2d.reshape(batch, seq, D_MODEL)
```

```python
import functools
import math

import jax
import jax.numpy as jnp
from jax import lax
from jax.experimental import pallas as pl
from jax.experimental.pallas import tpu as pltpu

F32 = jnp.float32
BF16 = jnp.bfloat16

D_MODEL = 2048
CHUNK = 64
D_SSM = 1024
SSD_HEADDIM = 64
SSD_HEADS = 16
SSD_GROUPS = 2
SSD_STATE = 128
SSD_CONV = 4
SSD_CONV_DIM = D_SSM + 2 * SSD_GROUPS * SSD_STATE
GROUP_W = D_SSM // SSD_GROUPS
MLA_V = 128
MLA_HEADS = 8
MLA_NOPE = 128
MLA_ROPE = 64
MLA_Q_RANK = 512
MLA_KV_RANK = 256
ROPE_THETA = 10000.0
D_FF = 5632
FFN_CONV = 3
PLE_DIM = 256
NORM_EPS = 1e-6

LANES = 128
SUBLANES = 8
QK_W = 2 * LANES
VMEM_LIMIT = 56 * 1024 * 1024
ROW_SPLIT = 2
IN_Z = 0
IN_XBC = IN_Z + D_SSM
IN_QA = IN_XBC + SSD_CONV_DIM
IN_CKV = IN_QA + MLA_Q_RANK
IN_KR = IN_CKV + MLA_KV_RANK
IN_DT = IN_KR + LANES
IN_N = IN_DT + LANES
DT_COPIES = 3


def _rms(xf):
    return xf * lax.rsqrt(jnp.mean(xf * xf, axis=-1, keepdims=True) + NORM_EPS)


def _silu(v):
    return v * jax.nn.sigmoid(v)


def _dot(a, b):
    return jnp.dot(a, b, preferred_element_type=F32)


def _dot_nt(a, b):
    return lax.dot_general(a, b, (((1,), (1,)), ((), ())), preferred_element_type=F32)


def _const_spec(shape):
    nd = len(shape)
    return pl.BlockSpec(shape, lambda *_: (0,) * nd, pipeline_mode=pl.Buffered(1))


def _params(sem, vmem=VMEM_LIMIT):
    return pltpu.CompilerParams(dimension_semantics=sem, vmem_limit_bytes=vmem)


def _rope_table_kernel(pos_ref, freq_ref, cos_ref, sin_ref):
    ang = pos_ref[...].astype(F32) * freq_ref[...]
    cos_ref[...] = jnp.cos(ang)
    sin_ref[...] = jnp.sin(ang)


def _rope_tables(positions):
    t = positions.size
    half = MLA_ROPE // 2
    inv_freq = ROPE_THETA ** (-jnp.arange(0, MLA_ROPE, 2, dtype=F32) / MLA_ROPE)
    rows = t * half // LANES
    pos_rep = jnp.repeat(positions.reshape(t), half).reshape(rows, LANES)
    freq = jnp.tile(inv_freq, LANES // half).reshape(1, LANES)
    cos, sin = pl.pallas_call(
        _rope_table_kernel,
        out_shape=(jax.ShapeDtypeStruct((rows, LANES), F32),) * 2,
        name="rope_table",
    )(pos_rep, freq)
    cos = cos.reshape(t, half)
    sin = sin.reshape(t, half)
    zeros = jnp.zeros((t, LANES - MLA_ROPE), F32)
    c2 = jnp.concatenate([cos, cos, zeros], axis=-1)
    s2 = jnp.concatenate([-sin, sin, zeros], axis=-1)
    return c2, s2, cos.T, sin.T


def _rope(r, c2, s2):
    half = MLA_ROPE // 2
    lane = lax.broadcasted_iota(jnp.int32, r.shape, 1)
    swapped = jnp.where(lane < half, pltpu.roll(r, LANES - half, 1), pltpu.roll(r, half, 1))
    return r * c2 + swapped * s2


def _in_proj_kernel(x_ref, nw_ref, w_ref, z_ref, xbc_ref, qa_ref, ckv_ref, kr_ref, dt_ref):
    sub = x_ref.shape[0] // ROW_SPLIT
    for r in range(ROW_SPLIT):
        rows = slice(r * sub, (r + 1) * sub)
        h = (_rms(x_ref[rows, :]) * nw_ref[...]).astype(BF16)

        def mm(lo, hi):
            return _dot(h, w_ref[:, lo:hi])

        z_ref[rows, :] = mm(IN_Z, IN_XBC).astype(BF16)
        xbc_ref[rows, :] = mm(IN_XBC, IN_QA).astype(BF16)
        qa_ref[rows, :] = mm(IN_QA, IN_CKV).astype(BF16)
        ckv_ref[rows, :] = mm(IN_CKV, IN_KR).astype(BF16)
        kr_ref[rows, :] = mm(IN_KR, IN_DT)
        dt_ref[rows, :] = mm(IN_DT, IN_N)


def _in_proj(x2d, norm_w, w_in_r, tm):
    t = x2d.shape[0]
    widths = (D_SSM, SSD_CONV_DIM, MLA_Q_RANK, MLA_KV_RANK, LANES, LANES)
    dtypes = (BF16, BF16, BF16, BF16, F32, F32)
    return pl.pallas_call(
        _in_proj_kernel,
        grid=(t // tm,),
        in_specs=[
            pl.BlockSpec((tm, D_MODEL), lambda i: (i, 0)),
            _const_spec((1, D_MODEL)),
            _const_spec((D_MODEL, IN_N)),
        ],
        out_specs=[pl.BlockSpec((tm, w), lambda i: (i, 0)) for w in widths],
        out_shape=[jax.ShapeDtypeStruct((t, w), d) for w, d in zip(widths, dtypes)],
        compiler_params=_params(("parallel",)),
        name="in_proj",
    )(x2d, norm_w, w_in_r)


def _ssd_kernel(z_ref, xbc_ref, dt_ref, cw_ref, cb_ref, dtb_ref, alog_ref, dsk_ref, nw_ref,
                cum_ref, e3_ref, e1_ref, y_ref, xpad_ref, state_ref, yacc_ref, *, ts):
    blk = pl.program_id(1)
    halo = SUBLANES

    @pl.when(blk == 0)
    def _():
        xpad_ref[0:halo, :] = jnp.zeros((halo, SSD_CONV_DIM), F32)
        state_ref[...] = jnp.zeros_like(state_ref)

    xpad_ref[halo:halo + ts, :] = xbc_ref[...].astype(F32)
    conv = cb_ref[...]
    for k in range(SSD_CONV):
        off = halo - (SSD_CONV - 1) + k
        conv = conv + cw_ref[k:k + 1, :] * xpad_ref[off:off + ts, :]
    xpad_ref[0:halo, :] = xpad_ref[ts:ts + halo, :]
    xc = _silu(conv)
    xs = xc[:, :D_SSM]
    bm = xc[:, D_SSM:D_SSM + SSD_GROUPS * SSD_STATE]
    cm = xc[:, D_SSM + SSD_GROUPS * SSD_STATE:]

    lane = lax.broadcasted_iota(jnp.int32, (ts, LANES), 1)
    raw = dt_ref[...] + dtb_ref[...]
    dt = jnp.maximum(raw, 0.0) + jnp.log1p(jnp.exp(-jnp.abs(raw)))
    a = jnp.where(lane < DT_COPIES * SSD_HEADS, -jnp.exp(alog_ref[...]), 0.0)
    adt = dt * a

    def split3(v):
        hi = v.astype(BF16).astype(F32)
        rem = v - hi
        mid = rem.astype(BF16).astype(F32)
        lo = rem - mid
        return jnp.where(lane < SSD_HEADS, hi, jnp.where(lane < 2 * SSD_HEADS, mid, lo)).astype(BF16)

    parts = _dot(cum_ref[...], split3(adt))
    cum = parts
    for shift in (SSD_HEADS, 2 * SSD_HEADS, LANES - SSD_HEADS, LANES - 2 * SSD_HEADS):
        cum = cum + pltpu.roll(parts, shift, 1)
    lane2 = lax.broadcasted_iota(jnp.int32, (2 * ts, LANES), 1)
    cum = jnp.where(lane2 < DT_COPIES * SSD_HEADS, cum, 0.0)
    acs, tot = cum[:ts], cum[ts:]
    eacs = jnp.exp(acs)
    dtdec = dt * jnp.exp(tot - acs)

    stacked = jnp.concatenate([dt, eacs, dtdec], axis=0).astype(BF16)
    expanded = _dot(stacked, e1_ref[...])
    dt_e, eacs_e, dtdec_e = expanded[:ts], expanded[ts:2 * ts], expanded[2 * ts:]

    col_part = _dot(split3(acs), e3_ref[...])

    xdt = (xs * dt_e).astype(BF16)
    xdec = (xs * dtdec_e).astype(BF16)
    bm16 = bm.astype(BF16)
    cm16 = cm.astype(BF16)

    li = lax.broadcasted_iota(jnp.int32, (CHUNK, D_SSM), 0)
    si = lax.broadcasted_iota(jnp.int32, (CHUNK, D_SSM), 1) % CHUNK
    heads_per_mm = 2 * LANES // SSD_HEADDIM
    bw = heads_per_mm * SSD_HEADDIM
    bd_r = lax.broadcasted_iota(jnp.int32, (bw, bw), 0) // CHUNK
    bd_c = lax.broadcasted_iota(jnp.int32, (bw, bw), 1) // SSD_HEADDIM
    bd_mask = bd_r == bd_c

    for c in range(ts // CHUNK):
        r0 = c * CHUNK
        rows = slice(r0, r0 + CHUNK)
        cp = col_part[rows]
        row_part = jnp.sum(jnp.where(li == si, cp, 0.0), axis=0, keepdims=True)
        lmat = jnp.exp(jnp.where(li >= si, cp - row_part, -jnp.inf))

        pieces = []
        for g in range(SSD_GROUPS):
            gl = slice(g * SSD_STATE, (g + 1) * SSD_STATE)
            gw = slice(g * GROUP_W, (g + 1) * GROUP_W)
            cg = cm16[rows, gl]
            bg = bm16[rows, gl]
            heads_g = GROUP_W // CHUNK
            cb_t = _dot_nt(cg, jnp.concatenate([bg] * heads_g, axis=0))
            m_g = (cb_t * lmat[:, gw]).astype(BF16)
            y_off = _dot(cg, state_ref[g].astype(BF16)) * eacs_e[rows, gw]
            y_diag = []
            for j in range(GROUP_W // bw):
                cols = slice(g * GROUP_W + j * bw, g * GROUP_W + (j + 1) * bw)
                xj = xdt[rows, cols]
                bd = jnp.where(bd_mask, jnp.concatenate([xj] * heads_per_mm, axis=0), jnp.zeros((), BF16))
                y_diag.append(_dot(m_g[:, j * bw:(j + 1) * bw], bd))
            pieces.append(jnp.concatenate(y_diag, axis=1) + y_off)
            bg_t = bm[rows, gl].T.astype(BF16)
            new_state = _dot(bg_t, xdec[rows, gw])
            decay = eacs_e[r0 + CHUNK - 1:r0 + CHUNK, gw]
            state_ref[g] = state_ref[g] * decay + new_state
        yacc_ref[rows, :] = jnp.concatenate(pieces, axis=1)

    y = yacc_ref[...] + dsk_ref[...] * xs
    y = y * _silu(z_ref[...].astype(F32))
    outs = []
    for g in range(SSD_GROUPS):
        outs.append(_rms(y[:, g * GROUP_W:(g + 1) * GROUP_W]))
    y_ref[...] = (jnp.concatenate(outs, axis=1) * nw_ref[...]).astype(BF16)


def _ssd(z, xbc, dt, conv_w, conv_b, dtb, alog, dsk, norm_w, batch, seq, ts):
    nblk = seq // ts
    r = jnp.arange(ts)
    same = (r[:, None] // CHUNK) == (r[None, :] // CHUNK)
    cum_mat = jnp.concatenate([same & (r[:, None] >= r[None, :]), same], axis=0).astype(BF16)
    lane_head = jnp.arange(D_SSM) // SSD_HEADDIM
    e_rows = (jnp.arange(LANES)[:, None] % SSD_HEADS) == lane_head[None, :]
    e1 = (e_rows & (jnp.arange(LANES)[:, None] < SSD_HEADS)).astype(BF16)
    e3 = (e_rows & (jnp.arange(LANES)[:, None] < DT_COPIES * SSD_HEADS)).astype(BF16)

    def row_spec(w):
        return pl.BlockSpec((ts, w), lambda b, j: (b * nblk + j, 0))

    return pl.pallas_call(
        functools.partial(_ssd_kernel, ts=ts),
        grid=(batch, nblk),
        in_specs=[
            row_spec(D_SSM), row_spec(SSD_CONV_DIM), row_spec(LANES),
            _const_spec((SSD_CONV, SSD_CONV_DIM)), _const_spec((1, SSD_CONV_DIM)),
            _const_spec((1, LANES)), _const_spec((1, LANES)),
            _const_spec((1, D_SSM)), _const_spec((1, D_SSM)),
            _const_spec((2 * ts, ts)), _const_spec((LANES, D_SSM)), _const_spec((LANES, D_SSM)),
        ],
        out_specs=row_spec(D_SSM),
        out_shape=jax.ShapeDtypeStruct((batch * seq, D_SSM), BF16),
        scratch_shapes=[
            pltpu.VMEM((ts + SUBLANES, SSD_CONV_DIM), F32),
            pltpu.VMEM((SSD_GROUPS, SSD_STATE, GROUP_W), F32),
            pltpu.VMEM((ts, D_SSM), F32),
        ],
        compiler_params=_params(("parallel", "arbitrary")),
        name="ssd",
    )(z, xbc, dt, conv_w, conv_b, dtb, alog, dsk, norm_w, cum_mat, e3, e1)


def _mla_proj_kernel(qa_ref, ckv_ref, kr_ref, c2_ref, s2_ref, cos_t_ref, sin_t_ref, qnw_ref, wq_t_ref,
                     kvnw_ref, wk_ref, wv_t_ref, q_t_ref, k_ref, v_t_ref):
    half = MLA_ROPE // 2
    scale = math.log2(math.e) / math.sqrt(MLA_NOPE + MLA_ROPE)
    qn = (_rms(qa_ref[...].astype(F32)) * qnw_ref[...]).astype(BF16)
    cn = (_rms(ckv_ref[...].astype(F32)) * kvnw_ref[...]).astype(BF16)

    q_t = _dot_nt(wq_t_ref[...], qn) * scale
    cos_t = cos_t_ref[...]
    sin_t = sin_t_ref[...]
    for h in range(MLA_HEADS):
        base = h * QK_W
        r1 = base + MLA_NOPE
        r2 = r1 + half
        t1 = q_t[r1:r2]
        t2 = q_t[r2:r2 + half]
        q_t_ref[base:r1, :] = q_t[base:r1].astype(BF16)
        q_t_ref[r1:r2, :] = (t1 * cos_t - t2 * sin_t).astype(BF16)
        q_t_ref[r2:r2 + half, :] = (t2 * cos_t + t1 * sin_t).astype(BF16)
        q_t_ref[r2 + half:base + QK_W, :] = jnp.zeros((QK_W - MLA_NOPE - MLA_ROPE, q_t.shape[1]), BF16)

    k_nope = _dot(cn, wk_ref[...])
    k_rope = _rope(kr_ref[...], c2_ref[...], s2_ref[...]).astype(BF16)
    for h in range(MLA_HEADS):
        base = h * QK_W
        k_ref[:, base:base + LANES] = k_nope[:, h * MLA_NOPE:(h + 1) * MLA_NOPE].astype(BF16)
        k_ref[:, base + LANES:base + QK_W] = k_rope

    v_t_ref[...] = _dot_nt(wv_t_ref[...], cn).astype(BF16)


def _mla_proj(qa, ckv, kr, c2, s2, cos_t, sin_t, qnw, wq_t, kvnw, wk, wv_t, tm):
    t = qa.shape[0]
    half = MLA_ROPE // 2

    def row_spec(w):
        return pl.BlockSpec((tm, w), lambda i: (i, 0))

    def col_spec(h):
        return pl.BlockSpec((h, tm), lambda i: (0, i))

    return pl.pallas_call(
        _mla_proj_kernel,
        grid=(t // tm,),
        in_specs=[
            row_spec(MLA_Q_RANK), row_spec(MLA_KV_RANK), row_spec(LANES), row_spec(LANES), row_spec(LANES),
            col_spec(half), col_spec(half),
            _const_spec((1, MLA_Q_RANK)), _const_spec((MLA_HEADS * QK_W, MLA_Q_RANK)),
            _const_spec((1, MLA_KV_RANK)), _const_spec((MLA_KV_RANK, MLA_HEADS * MLA_NOPE)),
            _const_spec((MLA_HEADS * MLA_V, MLA_KV_RANK)),
        ],
        out_specs=[col_spec(MLA_HEADS * QK_W), row_spec(MLA_HEADS * QK_W), col_spec(MLA_HEADS * MLA_V)],
        out_shape=[jax.ShapeDtypeStruct((MLA_HEADS * QK_W, t), BF16),
                   jax.ShapeDtypeStruct((t, MLA_HEADS * QK_W), BF16),
                   jax.ShapeDtypeStruct((MLA_HEADS * MLA_V, t), BF16)],
        compiler_params=_params(("parallel",)),
        name="mla_proj",
    )(qa, ckv, kr, c2, s2, cos_t, sin_t, qnw, wq_t, kvnw, wk, wv_t)


def _attention_kernel(q_t_ref, k_ref, v_t_ref, o_ref, s_ref, *, seq, tq):
    key_chunk = lax.broadcasted_iota(jnp.int32, (tq, tq), 0) // CHUNK
    query_chunk = lax.broadcasted_iota(jnp.int32, (tq, tq), 1) // CHUNK
    diag_mask = key_chunk <= query_chunk
    nq = seq // tq

    def score_block(i, j):
        keys = slice(j * tq, (j + 1) * tq)
        s = _dot(k_ref[keys, :], q_t_ref[:, i * tq:(i + 1) * tq])
        if j == i:
            s = jnp.where(diag_mask, s, -jnp.inf)
        s_ref[i % 2, keys, :] = s
        return jnp.max(s, axis=0, keepdims=True)

    def value_block(i, j, m):
        keys = slice(j * tq, (j + 1) * tq)
        p = jnp.exp2(s_ref[i % 2, keys, :] - m)
        return jnp.sum(p, axis=0, keepdims=True), _dot(v_t_ref[:, keys], p.astype(BF16))

    m = score_block(0, 0)
    for i in range(nq):
        denom = jnp.zeros((1, tq), F32)
        acc = jnp.zeros((MLA_V, tq), F32)
        m_next = None
        for j in range(i + 2):
            if i + 1 < nq:
                block_max = score_block(i + 1, j)
                m_next = block_max if m_next is None else jnp.maximum(m_next, block_max)
            if j <= i:
                psum, pv = value_block(i, j, m)
                denom = denom + psum
                acc = acc + pv
        o_ref[i * tq:(i + 1) * tq, :] = (acc / denom).T.astype(BF16)
        m = m_next


def _attention(q_t, k, v_t, batch, seq, tq):
    k3 = k.reshape(batch, seq, MLA_HEADS * QK_W)

    def feature_major(rows):
        return pl.BlockSpec((rows, seq), lambda b, h: (h, b))

    def token_major(w):
        return pl.BlockSpec((None, seq, w), lambda b, h: (b, 0, h))

    out = pl.pallas_call(
        functools.partial(_attention_kernel, seq=seq, tq=tq),
        grid=(batch, MLA_HEADS),
        in_specs=[feature_major(QK_W), token_major(QK_W), feature_major(MLA_V)],
        out_specs=token_major(MLA_V),
        out_shape=jax.ShapeDtypeStruct((batch, seq, MLA_HEADS * MLA_V), BF16),
        scratch_shapes=[pltpu.VMEM((2, seq, tq), F32)],
        compiler_params=_params(("parallel", "parallel")),
        name="attention",
    )(q_t, k3, v_t)
    return out.reshape(batch * seq, MLA_HEADS * MLA_V)


def _out_proj_kernel(x_ref, ys_ref, ym_ref, ws_ref, wm_ref, o_ref):
    o_ref[...] = x_ref[...] + _dot(ys_ref[...], ws_ref[...]) + _dot(ym_ref[...], wm_ref[...])


def _out_proj(x2d, y_ssd, y_mla, w_s, w_m, tm):
    t = x2d.shape[0]
    return pl.pallas_call(
        _out_proj_kernel,
        grid=(t // tm,),
        in_specs=[
            pl.BlockSpec((tm, D_MODEL), lambda i: (i, 0)),
            pl.BlockSpec((tm, D_SSM), lambda i: (i, 0)),
            pl.BlockSpec((tm, MLA_HEADS * MLA_V), lambda i: (i, 0)),
            _const_spec((D_SSM, D_MODEL)),
            _const_spec((MLA_HEADS * MLA_V, D_MODEL)),
        ],
        out_specs=pl.BlockSpec((tm, D_MODEL), lambda i: (i, 0)),
        out_shape=jax.ShapeDtypeStruct((t, D_MODEL), F32),
        compiler_params=_params(("parallel",)),
        name="out_proj",
    )(x2d, y_ssd, y_mla, w_s, w_m)


def _conv_ffn_kernel(x_ref, nw_ref, wg_ref, wu_ref, cwg_ref, cwu_ref, cbg_ref, cbu_ref, wd_ref, o_ref,
                     h_ref, acc_ref, gpad_ref, upad_ref, ghalo_ref, uhalo_ref, *, tm, tiles_per_seq):
    i = pl.program_id(0)
    f = pl.program_id(1)
    nf = pl.num_programs(1)
    halo = SUBLANES

    @pl.when(f == 0)
    def _():
        h_ref[...] = (_rms(x_ref[...]) * nw_ref[...]).astype(BF16)
        acc_ref[...] = jnp.zeros_like(acc_ref)

    h = h_ref[...]
    seq_start = (i % tiles_per_seq) == 0

    def conv(pad_ref, halo_ref, w_ref, cw_ref, cb_ref):
        prev = halo_ref[f]
        pad_ref[0:halo, :] = jnp.where(seq_start, jnp.zeros_like(prev), prev)
        pad_ref[halo:halo + tm, :] = _dot(h, w_ref[...])
        halo_ref[f] = pad_ref[tm:tm + halo, :]
        out = cb_ref[...]
        for k in range(FFN_CONV):
            off = halo - (FFN_CONV - 1) + k
            out = out + cw_ref[k:k + 1, :] * pad_ref[off:off + tm, :]
        return out

    gate = conv(gpad_ref, ghalo_ref, wg_ref, cwg_ref, cbg_ref)
    up = conv(upad_ref, uhalo_ref, wu_ref, cwu_ref, cbu_ref)
    act = (_silu(gate) * up).astype(BF16)
    acc_ref[...] += _dot(act, wd_ref[...])

    @pl.when(f == nf - 1)
    def _():
        o_ref[...] = x_ref[...] + acc_ref[...]


def _conv_ffn(x2d, norm_w, w_up, conv_w, conv_b, w_down, seq, tm, tf):
    t = x2d.shape[0]
    nf = D_FF // tf
    pad = pltpu.VMEM((tm + SUBLANES, tf), F32)
    carry = pltpu.VMEM((nf, SUBLANES, tf), F32)
    return pl.pallas_call(
        functools.partial(_conv_ffn_kernel, tm=tm, tiles_per_seq=seq // tm),
        grid=(t // tm, nf),
        in_specs=[
            pl.BlockSpec((tm, D_MODEL), lambda i, f: (i, 0)),
            _const_spec((1, D_MODEL)),
            pl.BlockSpec((D_MODEL, tf), lambda i, f: (0, f)),
            pl.BlockSpec((D_MODEL, tf), lambda i, f: (0, f + nf)),
            pl.BlockSpec((FFN_CONV, tf), lambda i, f: (0, f)),
            pl.BlockSpec((FFN_CONV, tf), lambda i, f: (0, f + nf)),
            pl.BlockSpec((1, tf), lambda i, f: (0, f)),
            pl.BlockSpec((1, tf), lambda i, f: (0, f + nf)),
            pl.BlockSpec((tf, D_MODEL), lambda i, f: (f, 0)),
        ],
        out_specs=pl.BlockSpec((tm, D_MODEL), lambda i, f: (i, 0)),
        out_shape=jax.ShapeDtypeStruct((t, D_MODEL), F32),
        scratch_shapes=[
            pltpu.VMEM((tm, D_MODEL), BF16),
            pltpu.VMEM((tm, D_MODEL), F32),
            pad, pad,
            carry, carry,
        ],
        compiler_params=_params(("arbitrary", "arbitrary")),
        name="conv_ffn",
    )(x2d, norm_w, w_up, w_up, conv_w, conv_w, conv_b, conv_b, w_down)


def _ple_final_kernel(x_ref, p_ref, nw_ref, wg_ref, bg_ref, wp_ref, pnw_ref, fnw_ref, o_ref):
    sub = x_ref.shape[0] // ROW_SPLIT
    for r in range(ROW_SPLIT):
        rows = slice(r * sub, (r + 1) * sub)
        x = x_ref[rows, :]
        hn = (_rms(x) * nw_ref[...]).astype(BF16)
        gate = jax.nn.sigmoid(_dot(hn, wg_ref[...]) + bg_ref[...])
        proj = _dot(p_ref[rows, :].astype(BF16), wp_ref[...])
        x = x + gate * (_rms(proj) * pnw_ref[...])
        o_ref[rows, :] = _rms(x) * fnw_ref[...]


def _ple_final(x2d, p2d, norm_w, w_gate, b_gate, w_proj, post_w, final_w, tm):
    t = x2d.shape[0]
    return pl.pallas_call(
        _ple_final_kernel,
        grid=(t // tm,),
        in_specs=[
            pl.BlockSpec((tm, D_MODEL), lambda i: (i, 0)),
            pl.BlockSpec((tm, PLE_DIM), lambda i: (i, 0)),
            _const_spec((1, D_MODEL)),
            _const_spec((D_MODEL, D_MODEL)),
            _const_spec((1, D_MODEL)),
            _const_spec((PLE_DIM, D_MODEL)),
            _const_spec((1, D_MODEL)),
            _const_spec((1, D_MODEL)),
        ],
        out_specs=pl.BlockSpec((tm, D_MODEL), lambda i: (i, 0)),
        out_shape=jax.ShapeDtypeStruct((t, D_MODEL), F32),
        compiler_params=_params(("parallel",)),
        name="ple_final",
    )(x2d, p2d, norm_w, w_gate, b_gate, w_proj, post_w, final_w)


def _pad_cols(w, width):
    return jnp.pad(w, ((0, 0), (0, width - w.shape[1])))


def _prep_w_in(w_in):
    o_xbc = D_SSM
    o_dt = o_xbc + SSD_CONV_DIM
    o_qa = o_dt + SSD_HEADS
    o_kv = o_qa + MLA_Q_RANK
    o_kr = o_kv + MLA_KV_RANK
    w_dt = w_in[:, o_dt:o_qa]
    return jnp.concatenate([
        w_in[:, :o_dt],
        w_in[:, o_qa:o_kr],
        _pad_cols(w_in[:, o_kr:], LANES),
        _pad_cols(jnp.concatenate([w_dt] * DT_COPIES, axis=1), LANES),
    ], axis=1).astype(BF16)


def _prep_w_q_t(w_q_b):
    w = w_q_b.reshape(MLA_Q_RANK, MLA_HEADS, MLA_NOPE + MLA_ROPE)
    w = jnp.pad(w, ((0, 0), (0, 0), (0, QK_W - MLA_NOPE - MLA_ROPE)))
    return w.reshape(MLA_Q_RANK, MLA_HEADS * QK_W).T.astype(BF16)


def _prep_w_kv(w_kv_b):
    w = w_kv_b.reshape(MLA_KV_RANK, MLA_HEADS, MLA_NOPE + MLA_V)
    w_k = w[:, :, :MLA_NOPE].reshape(MLA_KV_RANK, MLA_HEADS * MLA_NOPE)
    w_v = w[:, :, MLA_NOPE:].reshape(MLA_KV_RANK, MLA_HEADS * MLA_V)
    return w_k.astype(BF16), w_v.T.astype(BF16)


def _lane_row(v, width=None):
    v = v.reshape(1, -1).astype(F32)
    return v if width is None else _pad_cols(v, width)


def kernel(x, p, positions, mix_norm_w, w_in, conv_w, conv_b, dt_bias, a_log, d_skip, ssd_norm_w,
           q_a_norm_w, w_q_b, kv_a_norm_w, w_kv_b, w_out, ffn_norm_w, w_ffn_up, ffn_conv_w,
           ffn_conv_b, w_ffn_down, ple_norm_w, w_ple_gate, b_ple_gate, w_ple_proj, ple_post_norm_w,
           final_norm_w):
    batch, seq, _ = x.shape
    depth = w_in.shape[0]
    t = batch * seq
    x2d = x.reshape(t, D_MODEL)
    c2, s2, cos_t, sin_t = _rope_tables(positions)

    assert depth == 1, "the final RMSNorm is fused into the single layer's last kernel"
    i = 0
    z, xbc, qa, ckv, kr, dt = _in_proj(x2d, _lane_row(mix_norm_w[i]), _prep_w_in(w_in[i]), tm=512)
    y_ssd = _ssd(
        z, xbc, dt, conv_w[i], _lane_row(conv_b[i]),
        _lane_row(jnp.tile(dt_bias[i], DT_COPIES), LANES),
        _lane_row(jnp.tile(a_log[i], DT_COPIES), LANES),
        _lane_row(jnp.repeat(d_skip[i], SSD_HEADDIM)), _lane_row(ssd_norm_w[i]),
        batch, seq, ts=256)
    w_k, w_v_t = _prep_w_kv(w_kv_b[i])
    q_t, k, v_t = _mla_proj(qa, ckv, kr, c2, s2, cos_t, sin_t, _lane_row(q_a_norm_w[i]),
                            _prep_w_q_t(w_q_b[i]), _lane_row(kv_a_norm_w[i]), w_k, w_v_t, tm=512)
    y_mla = _attention(q_t, k, v_t, batch, seq, tq=256)
    w_o = w_out[i].astype(BF16)
    x2d = _out_proj(x2d, y_ssd, y_mla, w_o[:D_SSM], w_o[D_SSM:], tm=512)
    x2d = _conv_ffn(x2d, _lane_row(ffn_norm_w[i]), w_ffn_up[i].astype(BF16), ffn_conv_w[i],
                    _lane_row(ffn_conv_b[i]), w_ffn_down[i].astype(BF16), seq, tm=512, tf=512)
    x2d = _ple_final(x2d, p[i].reshape(t, PLE_DIM), _lane_row(ple_norm_w[i]),
                     w_ple_gate[i].astype(BF16), _lane_row(b_ple_gate[i]), w_ple_proj[i].astype(BF16),
                     _lane_row(ple_post_norm_w[i]), _lane_row(final_norm_w), tm=512)
    return x2d.reshape(batch, seq, D_MODEL)
```

```python
import functools
import math

import jax
import jax.numpy as jnp
from jax import lax
from jax.experimental import pallas as pl
from jax.experimental.pallas import tpu as pltpu

F32 = jnp.float32
BF16 = jnp.bfloat16

D_MODEL = 2048
CHUNK = 64
D_SSM = 1024
SSD_HEADDIM = 64
SSD_HEADS = 16
SSD_GROUPS = 2
SSD_STATE = 128
SSD_CONV = 4
SSD_CONV_DIM = D_SSM + 2 * SSD_GROUPS * SSD_STATE
GROUP_W = D_SSM // SSD_GROUPS
MLA_V = 128
MLA_HEADS = 8
MLA_NOPE = 128
MLA_ROPE = 64
MLA_Q_RANK = 512
MLA_KV_RANK = 256
ROPE_THETA = 10000.0
D_FF = 5632
FFN_CONV = 3
PLE_DIM = 256
NORM_EPS = 1e-6

LANES = 128
SUBLANES = 8
QK_W = 2 * LANES
VMEM_LIMIT = 56 * 1024 * 1024
ROW_SPLIT = 2
IN_Z = 0
IN_XBC = IN_Z + D_SSM
IN_QA = IN_XBC + SSD_CONV_DIM
IN_CKV = IN_QA + MLA_Q_RANK
IN_KR = IN_CKV + MLA_KV_RANK
IN_DT = IN_KR + LANES
IN_N = IN_DT + LANES
DT_COPIES = 3


def _rms(xf):
    return xf * lax.rsqrt(jnp.mean(xf * xf, axis=-1, keepdims=True) + NORM_EPS)


def _silu(v):
    return v * jax.nn.sigmoid(v)


def _dot(a, b):
    return jnp.dot(a, b, preferred_element_type=F32)


def _dot_nt(a, b):
    return lax.dot_general(a, b, (((1,), (1,)), ((), ())), preferred_element_type=F32)


def _const_spec(shape):
    nd = len(shape)
    return pl.BlockSpec(shape, lambda *_: (0,) * nd, pipeline_mode=pl.Buffered(1))


def _params(sem, vmem=VMEM_LIMIT):
    return pltpu.CompilerParams(dimension_semantics=sem, vmem_limit_bytes=vmem)


def _rope_table_kernel(pos_ref, freq_ref, cos_ref, sin_ref):
    ang = pos_ref[...].astype(F32) * freq_ref[...]
    cos_ref[...] = jnp.cos(ang)
    sin_ref[...] = jnp.sin(ang)


def _rope_tables(positions):
    t = positions.size
    half = MLA_ROPE // 2
    inv_freq = ROPE_THETA ** (-jnp.arange(0, MLA_ROPE, 2, dtype=F32) / MLA_ROPE)
    rows = t * half // LANES
    pos_rep = jnp.repeat(positions.reshape(t), half).reshape(rows, LANES)
    freq = jnp.tile(inv_freq, LANES // half).reshape(1, LANES)
    cos, sin = pl.pallas_call(
        _rope_table_kernel,
        out_shape=(jax.ShapeDtypeStruct((rows, LANES), F32),) * 2,
        name="rope_table",
    )(pos_rep, freq)
    cos = cos.reshape(t, half)
    sin = sin.reshape(t, half)
    zeros = jnp.zeros((t, LANES - MLA_ROPE), F32)
    c2 = jnp.concatenate([cos, cos, zeros], axis=-1)
    s2 = jnp.concatenate([-sin, sin, zeros], axis=-1)
    return c2, s2, cos.T, sin.T


def _rope(r, c2, s2):
    half = MLA_ROPE // 2
    lane = lax.broadcasted_iota(jnp.int32, r.shape, 1)
    swapped = jnp.where(lane < half, pltpu.roll(r, LANES - half, 1), pltpu.roll(r, half, 1))
    return r * c2 + swapped * s2


def _in_proj_kernel(x_ref, nw_ref, w_ref, z_ref, xbc_ref, qa_ref, ckv_ref, kr_ref, dt_ref):
    sub = x_ref.shape[0] // ROW_SPLIT
    for r in range(ROW_SPLIT):
        rows = slice(r * sub, (r + 1) * sub)
        h = (_rms(x_ref[rows, :]) * nw_ref[...]).astype(BF16)

        def mm(lo, hi):
            return _dot(h, w_ref[:, lo:hi])

        z_ref[rows, :] = mm(IN_Z, IN_XBC).astype(BF16)
        xbc_ref[rows, :] = mm(IN_XBC, IN_QA).astype(BF16)
        qa_ref[rows, :] = mm(IN_QA, IN_CKV).astype(BF16)
        ckv_ref[rows, :] = mm(IN_CKV, IN_KR).astype(BF16)
        kr_ref[rows, :] = mm(IN_KR, IN_DT)
        dt_ref[rows, :] = mm(IN_DT, IN_N)


def _in_proj(x2d, norm_w, w_in_r, tm):
    t = x2d.shape[0]
    widths = (D_SSM, SSD_CONV_DIM, MLA_Q_RANK, MLA_KV_RANK, LANES, LANES)
    dtypes = (BF16, BF16, BF16, BF16, F32, F32)
    return pl.pallas_call(
        _in_proj_kernel,
        grid=(t // tm,),
        in_specs=[
            pl.BlockSpec((tm, D_MODEL), lambda i: (i, 0)),
            _const_spec((1, D_MODEL)),
            _const_spec((D_MODEL, IN_N)),
        ],
        out_specs=[pl.BlockSpec((tm, w), lambda i: (i, 0)) for w in widths],
        out_shape=[jax.ShapeDtypeStruct((t, w), d) for w, d in zip(widths, dtypes)],
        compiler_params=_params(("parallel",)),
        name="in_proj",
    )(x2d, norm_w, w_in_r)


def _ssd_kernel(z_ref, xbc_ref, dt_ref, cw_ref, cb_ref, dtb_ref, alog_ref, dsk_ref, nw_ref,
                shift_ref, cum_ref, e3_ref, e1_ref, y_ref, xpad_ref, state_ref, yacc_ref, *, ts):
    blk = pl.program_id(1)
    halo = SUBLANES

    @pl.when(blk == 0)
    def _():
        xpad_ref[0:halo, :] = jnp.zeros((halo, SSD_CONV_DIM), F32)
        state_ref[...] = jnp.zeros_like(state_ref)

    x16 = xbc_ref[...]
    xpad_ref[halo:2 * halo, :] = x16[0:halo].astype(F32)
    head = cb_ref[...]
    for k in range(SSD_CONV):
        off = halo - (SSD_CONV - 1) + k
        head = head + cw_ref[k:k + 1, :] * xpad_ref[off:off + halo, :]
    xpad_ref[0:halo, :] = x16[ts - halo:ts].astype(F32)
    delayed = _dot(shift_ref[...], x16)
    body = cb_ref[...] + cw_ref[SSD_CONV - 1:SSD_CONV, :] * x16.astype(F32)
    for d in range(1, SSD_CONV):
        k = SSD_CONV - 1 - d
        body = body + cw_ref[k:k + 1, :] * delayed[(d - 1) * ts:d * ts]
    xc = _silu(jnp.concatenate([head, body[halo:]], axis=0))
    xs = xc[:, :D_SSM]
    bm = xc[:, D_SSM:D_SSM + SSD_GROUPS * SSD_STATE]
    cm = xc[:, D_SSM + SSD_GROUPS * SSD_STATE:]

    lane = lax.broadcasted_iota(jnp.int32, (ts, LANES), 1)
    raw = dt_ref[...] + dtb_ref[...]
    dt = jnp.maximum(raw, 0.0) + jnp.log1p(jnp.exp(-jnp.abs(raw)))
    a = jnp.where(lane < DT_COPIES * SSD_HEADS, -jnp.exp(alog_ref[...]), 0.0)
    adt = dt * a

    def split3(v):
        hi = v.astype(BF16).astype(F32)
        rem = v - hi
        mid = rem.astype(BF16).astype(F32)
        lo = rem - mid
        return jnp.where(lane < SSD_HEADS, hi, jnp.where(lane < 2 * SSD_HEADS, mid, lo)).astype(BF16)

    parts = _dot(cum_ref[...], split3(adt))
    cum = parts
    for shift in (SSD_HEADS, 2 * SSD_HEADS, LANES - SSD_HEADS, LANES - 2 * SSD_HEADS):
        cum = cum + pltpu.roll(parts, shift, 1)
    lane2 = lax.broadcasted_iota(jnp.int32, (2 * ts, LANES), 1)
    cum = jnp.where(lane2 < DT_COPIES * SSD_HEADS, cum, 0.0)
    acs, tot = cum[:ts], cum[ts:]
    eacs = jnp.exp(acs)
    dtdec = dt * jnp.exp(tot - acs)

    stacked = jnp.concatenate([dt, eacs, dtdec], axis=0).astype(BF16)
    expanded = _dot(stacked, e1_ref[...])
    dt_e, eacs_e, dtdec_e = expanded[:ts], expanded[ts:2 * ts], expanded[2 * ts:]

    col_part = _dot(split3(acs), e3_ref[...])

    xdt = (xs * dt_e).astype(BF16)
    xdec = (xs * dtdec_e).astype(BF16)
    bm16 = bm.astype(BF16)
    cm16 = cm.astype(BF16)

    li = lax.broadcasted_iota(jnp.int32, (CHUNK, D_SSM), 0)
    si = lax.broadcasted_iota(jnp.int32, (CHUNK, D_SSM), 1) % CHUNK
    heads_per_mm = 2 * LANES // SSD_HEADDIM
    bw = heads_per_mm * SSD_HEADDIM
    bd_r = lax.broadcasted_iota(jnp.int32, (bw, bw), 0) // CHUNK
    bd_c = lax.broadcasted_iota(jnp.int32, (bw, bw), 1) // SSD_HEADDIM
    bd_mask = bd_r == bd_c

    for c in range(ts // CHUNK):
        r0 = c * CHUNK
        rows = slice(r0, r0 + CHUNK)
        cp = col_part[rows]
        row_part = jnp.sum(jnp.where(li == si, cp, 0.0), axis=0, keepdims=True)
        lmat = jnp.exp(jnp.where(li >= si, cp - row_part, -jnp.inf))

        pieces = []
        for g in range(SSD_GROUPS):
            gl = slice(g * SSD_STATE, (g + 1) * SSD_STATE)
            gw = slice(g * GROUP_W, (g + 1) * GROUP_W)
            cg = cm16[rows, gl]
            bg = bm16[rows, gl]
            heads_g = GROUP_W // CHUNK
            cb_t = _dot_nt(cg, jnp.concatenate([bg] * heads_g, axis=0))
            m_g = (cb_t * lmat[:, gw]).astype(BF16)
            y_off = _dot(cg, state_ref[g].astype(BF16)) * eacs_e[rows, gw]
            y_diag = []
            for j in range(GROUP_W // bw):
                cols = slice(g * GROUP_W + j * bw, g * GROUP_W + (j + 1) * bw)
                xj = xdt[rows, cols]
                bd = jnp.where(bd_mask, jnp.concatenate([xj] * heads_per_mm, axis=0), jnp.zeros((), BF16))
                y_diag.append(_dot(m_g[:, j * bw:(j + 1) * bw], bd))
            pieces.append(jnp.concatenate(y_diag, axis=1) + y_off)
            bg_t = bm[rows, gl].T.astype(BF16)
            new_state = _dot(bg_t, xdec[rows, gw])
            decay = eacs_e[r0 + CHUNK - 1:r0 + CHUNK, gw]
            state_ref[g] = state_ref[g] * decay + new_state
        yacc_ref[rows, :] = jnp.concatenate(pieces, axis=1)

    y = yacc_ref[...] + dsk_ref[...] * xs
    y = y * _silu(z_ref[...].astype(F32))
    outs = []
    for g in range(SSD_GROUPS):
        outs.append(_rms(y[:, g * GROUP_W:(g + 1) * GROUP_W]))
    y_ref[...] = (jnp.concatenate(outs, axis=1) * nw_ref[...]).astype(BF16)


def _ssd(z, xbc, dt, conv_w, conv_b, dtb, alog, dsk, norm_w, batch, seq, ts):
    nblk = seq // ts
    r = jnp.arange(ts)
    same = (r[:, None] // CHUNK) == (r[None, :] // CHUNK)
    cum_mat = jnp.concatenate([same & (r[:, None] >= r[None, :]), same], axis=0).astype(BF16)
    shift_mat = jnp.concatenate([r[:, None] - d == r[None, :] for d in range(1, SSD_CONV)], axis=0).astype(BF16)
    lane_head = jnp.arange(D_SSM) // SSD_HEADDIM
    e_rows = (jnp.arange(LANES)[:, None] % SSD_HEADS) == lane_head[None, :]
    e1 = (e_rows & (jnp.arange(LANES)[:, None] < SSD_HEADS)).astype(BF16)
    e3 = (e_rows & (jnp.arange(LANES)[:, None] < DT_COPIES * SSD_HEADS)).astype(BF16)

    def row_spec(w):
        return pl.BlockSpec((ts, w), lambda b, j: (b * nblk + j, 0))

    return pl.pallas_call(
        functools.partial(_ssd_kernel, ts=ts),
        grid=(batch, nblk),
        in_specs=[
            row_spec(D_SSM), row_spec(SSD_CONV_DIM), row_spec(LANES),
            _const_spec((SSD_CONV, SSD_CONV_DIM)), _const_spec((1, SSD_CONV_DIM)),
            _const_spec((1, LANES)), _const_spec((1, LANES)),
            _const_spec((1, D_SSM)), _const_spec((1, D_SSM)),
            _const_spec(((SSD_CONV - 1) * ts, ts)),
            _const_spec((2 * ts, ts)), _const_spec((LANES, D_SSM)), _const_spec((LANES, D_SSM)),
        ],
        out_specs=row_spec(D_SSM),
        out_shape=jax.ShapeDtypeStruct((batch * seq, D_SSM), BF16),
        scratch_shapes=[
            pltpu.VMEM((2 * SUBLANES, SSD_CONV_DIM), F32),
            pltpu.VMEM((SSD_GROUPS, SSD_STATE, GROUP_W), F32),
            pltpu.VMEM((ts, D_SSM), F32),
        ],
        compiler_params=_params(("parallel", "arbitrary")),
        name="ssd",
    )(z, xbc, dt, conv_w, conv_b, dtb, alog, dsk, norm_w, shift_mat, cum_mat, e3, e1)


def _mla_proj_kernel(qa_ref, ckv_ref, kr_ref, c2_ref, s2_ref, cos_t_ref, sin_t_ref, qnw_ref, wq_t_ref,
                     kvnw_ref, wk_ref, wv_t_ref, q_t_ref, k_ref, v_t_ref):
    half = MLA_ROPE // 2
    scale = math.log2(math.e) / math.sqrt(MLA_NOPE + MLA_ROPE)
    qn = (_rms(qa_ref[...].astype(F32)) * qnw_ref[...]).astype(BF16)
    cn = (_rms(ckv_ref[...].astype(F32)) * kvnw_ref[...]).astype(BF16)

    q_t = _dot_nt(wq_t_ref[...], qn) * scale
    cos_t = cos_t_ref[...]
    sin_t = sin_t_ref[...]
    for h in range(MLA_HEADS):
        base = h * QK_W
        r1 = base + MLA_NOPE
        r2 = r1 + half
        t1 = q_t[r1:r2]
        t2 = q_t[r2:r2 + half]
        q_t_ref[base:r1, :] = q_t[base:r1].astype(BF16)
        q_t_ref[r1:r2, :] = (t1 * cos_t - t2 * sin_t).astype(BF16)
        q_t_ref[r2:r2 + half, :] = (t2 * cos_t + t1 * sin_t).astype(BF16)
        q_t_ref[r2 + half:base + QK_W, :] = jnp.zeros((QK_W - MLA_NOPE - MLA_ROPE, q_t.shape[1]), BF16)

    k_nope = _dot(cn, wk_ref[...])
    k_rope = _rope(kr_ref[...], c2_ref[...], s2_ref[...]).astype(BF16)
    for h in range(MLA_HEADS):
        base = h * QK_W
        k_ref[:, base:base + LANES] = k_nope[:, h * MLA_NOPE:(h + 1) * MLA_NOPE].astype(BF16)
        k_ref[:, base + LANES:base + QK_W] = k_rope

    v_t_ref[...] = _dot_nt(wv_t_ref[...], cn).astype(BF16)


def _mla_proj(qa, ckv, kr, c2, s2, cos_t, sin_t, qnw, wq_t, kvnw, wk, wv_t, tm):
    t = qa.shape[0]
    half = MLA_ROPE // 2

    def row_spec(w):
        return pl.BlockSpec((tm, w), lambda i: (i, 0))

    def col_spec(h):
        return pl.BlockSpec((h, tm), lambda i: (0, i))

    return pl.pallas_call(
        _mla_proj_kernel,
        grid=(t // tm,),
        in_specs=[
            row_spec(MLA_Q_RANK), row_spec(MLA_KV_RANK), row_spec(LANES), row_spec(LANES), row_spec(LANES),
            col_spec(half), col_spec(half),
            _const_spec((1, MLA_Q_RANK)), _const_spec((MLA_HEADS * QK_W, MLA_Q_RANK)),
            _const_spec((1, MLA_KV_RANK)), _const_spec((MLA_KV_RANK, MLA_HEADS * MLA_NOPE)),
            _const_spec((MLA_HEADS * MLA_V, MLA_KV_RANK)),
        ],
        out_specs=[col_spec(MLA_HEADS * QK_W), row_spec(MLA_HEADS * QK_W), col_spec(MLA_HEADS * MLA_V)],
        out_shape=[jax.ShapeDtypeStruct((MLA_HEADS * QK_W, t), BF16),
                   jax.ShapeDtypeStruct((t, MLA_HEADS * QK_W), BF16),
                   jax.ShapeDtypeStruct((MLA_HEADS * MLA_V, t), BF16)],
        compiler_params=_params(("parallel",)),
        name="mla_proj",
    )(qa, ckv, kr, c2, s2, cos_t, sin_t, qnw, wq_t, kvnw, wk, wv_t)


def _attention_kernel(q_t_ref, k_ref, v_t_ref, o_ref, s_ref, *, seq, tq):
    key_chunk = lax.broadcasted_iota(jnp.int32, (tq, tq), 0) // CHUNK
    query_chunk = lax.broadcasted_iota(jnp.int32, (tq, tq), 1) // CHUNK
    diag_mask = key_chunk <= query_chunk
    nq = seq // tq

    def score_block(i, j):
        keys = slice(j * tq, (j + 1) * tq)
        s = _dot(k_ref[keys, :], q_t_ref[:, i * tq:(i + 1) * tq])
        if j == i:
            s = jnp.where(diag_mask, s, -jnp.inf)
        s_ref[i % 2, keys, :] = s
        return jnp.max(s, axis=0, keepdims=True)

    def value_block(i, j, m):
        keys = slice(j * tq, (j + 1) * tq)
        p = jnp.exp2(s_ref[i % 2, keys, :] - m)
        return jnp.sum(p, axis=0, keepdims=True), _dot(v_t_ref[:, keys], p.astype(BF16))

    m = score_block(0, 0)
    for i in range(nq):
        denom = jnp.zeros((1, tq), F32)
        acc = jnp.zeros((MLA_V, tq), F32)
        m_next = None
        for j in range(i + 2):
            if i + 1 < nq:
                block_max = score_block(i + 1, j)
                m_next = block_max if m_next is None else jnp.maximum(m_next, block_max)
            if j <= i:
                psum, pv = value_block(i, j, m)
                denom = denom + psum
                acc = acc + pv
        o_ref[i * tq:(i + 1) * tq, :] = (acc / denom).T.astype(BF16)
        m = m_next


def _attention(q_t, k, v_t, batch, seq, tq):
    k3 = k.reshape(batch, seq, MLA_HEADS * QK_W)

    def feature_major(rows):
        return pl.BlockSpec((rows, seq), lambda b, h: (h, b))

    def token_major(w):
        return pl.BlockSpec((None, seq, w), lambda b, h: (b, 0, h))

    out = pl.pallas_call(
        functools.partial(_attention_kernel, seq=seq, tq=tq),
        grid=(batch, MLA_HEADS),
        in_specs=[feature_major(QK_W), token_major(QK_W), feature_major(MLA_V)],
        out_specs=token_major(MLA_V),
        out_shape=jax.ShapeDtypeStruct((batch, seq, MLA_HEADS * MLA_V), BF16),
        scratch_shapes=[pltpu.VMEM((2, seq, tq), F32)],
        compiler_params=_params(("parallel", "parallel")),
        name="attention",
    )(q_t, k3, v_t)
    return out.reshape(batch * seq, MLA_HEADS * MLA_V)


def _out_proj_kernel(x_ref, ys_ref, ym_ref, ws_ref, wm_ref, o_ref):
    o_ref[...] = x_ref[...] + _dot(ys_ref[...], ws_ref[...]) + _dot(ym_ref[...], wm_ref[...])


def _out_proj(x2d, y_ssd, y_mla, w_o, tm):
    t = x2d.shape[0]
    assert D_SSM == MLA_HEADS * MLA_V

    def half_spec(j):
        return pl.BlockSpec((D_SSM, D_MODEL), lambda i: (j, 0), pipeline_mode=pl.Buffered(1))

    return pl.pallas_call(
        _out_proj_kernel,
        grid=(t // tm,),
        in_specs=[
            pl.BlockSpec((tm, D_MODEL), lambda i: (i, 0)),
            pl.BlockSpec((tm, D_SSM), lambda i: (i, 0)),
            pl.BlockSpec((tm, MLA_HEADS * MLA_V), lambda i: (i, 0)),
            half_spec(0),
            half_spec(1),
        ],
        out_specs=pl.BlockSpec((tm, D_MODEL), lambda i: (i, 0)),
        out_shape=jax.ShapeDtypeStruct((t, D_MODEL), F32),
        compiler_params=_params(("parallel",)),
        name="out_proj",
    )(x2d, y_ssd, y_mla, w_o, w_o)


def _conv_ffn_kernel(x_ref, nw_ref, wg_ref, wu_ref, cwg_ref, cwu_ref, cbg_ref, cbu_ref, wd_ref, o_ref,
                     h_ref, acc_ref, gpad_ref, upad_ref, ghalo_ref, uhalo_ref, *, tm, tiles_per_seq):
    i = pl.program_id(0)
    f = pl.program_id(1)
    nf = pl.num_programs(1)
    halo = SUBLANES

    @pl.when(f == 0)
    def _():
        h_ref[...] = (_rms(x_ref[...]) * nw_ref[...]).astype(BF16)
        acc_ref[...] = jnp.zeros_like(acc_ref)

    h = h_ref[...]
    seq_start = (i % tiles_per_seq) == 0

    def conv(pad_ref, halo_ref, w_ref, cw_ref, cb_ref):
        prev = halo_ref[f]
        pad_ref[0:halo, :] = jnp.where(seq_start, jnp.zeros_like(prev), prev)
        pad_ref[halo:halo + tm, :] = _dot(h, w_ref[...])
        halo_ref[f] = pad_ref[tm:tm + halo, :]
        out = cb_ref[...]
        for k in range(FFN_CONV):
            off = halo - (FFN_CONV - 1) + k
            out = out + cw_ref[k:k + 1, :] * pad_ref[off:off + tm, :]
        return out

    gate = conv(gpad_ref, ghalo_ref, wg_ref, cwg_ref, cbg_ref)
    up = conv(upad_ref, uhalo_ref, wu_ref, cwu_ref, cbu_ref)
    act = (_silu(gate) * up).astype(BF16)
    acc_ref[...] += _dot(act, wd_ref[...])

    @pl.when(f == nf - 1)
    def _():
        o_ref[...] = x_ref[...] + acc_ref[...]


def _conv_ffn(x2d, norm_w, w_up, conv_w, conv_b, w_down, seq, tm, tf):
    t = x2d.shape[0]
    nf = D_FF // tf
    pad = pltpu.VMEM((tm + SUBLANES, tf), F32)
    carry = pltpu.VMEM((nf, SUBLANES, tf), F32)
    return pl.pallas_call(
        functools.partial(_conv_ffn_kernel, tm=tm, tiles_per_seq=seq // tm),
        grid=(t // tm, nf),
        in_specs=[
            pl.BlockSpec((tm, D_MODEL), lambda i, f: (i, 0)),
            _const_spec((1, D_MODEL)),
            pl.BlockSpec((D_MODEL, tf), lambda i, f: (0, f)),
            pl.BlockSpec((D_MODEL, tf), lambda i, f: (0, f + nf)),
            pl.BlockSpec((FFN_CONV, tf), lambda i, f: (0, f)),
            pl.BlockSpec((FFN_CONV, tf), lambda i, f: (0, f + nf)),
            pl.BlockSpec((1, tf), lambda i, f: (0, f)),
            pl.BlockSpec((1, tf), lambda i, f: (0, f + nf)),
            pl.BlockSpec((tf, D_MODEL), lambda i, f: (f, 0)),
        ],
        out_specs=pl.BlockSpec((tm, D_MODEL), lambda i, f: (i, 0)),
        out_shape=jax.ShapeDtypeStruct((t, D_MODEL), F32),
        scratch_shapes=[
            pltpu.VMEM((tm, D_MODEL), BF16),
            pltpu.VMEM((tm, D_MODEL), F32),
            pad, pad,
            carry, carry,
        ],
        compiler_params=_params(("arbitrary", "arbitrary")),
        name="conv_ffn",
    )(x2d, norm_w, w_up, w_up, conv_w, conv_w, conv_b, conv_b, w_down)


def _ple_final_kernel(x_ref, p_ref, nw_ref, wg_ref, bg_ref, wp_ref, pnw_ref, fnw_ref, o_ref):
    sub = x_ref.shape[0] // ROW_SPLIT
    for r in range(ROW_SPLIT):
        rows = slice(r * sub, (r + 1) * sub)
        x = x_ref[rows, :]
        hn = (_rms(x) * nw_ref[...]).astype(BF16)
        gate = jax.nn.sigmoid(_dot(hn, wg_ref[...]) + bg_ref[...])
        proj = _dot(p_ref[rows, :].astype(BF16), wp_ref[...])
        x = x + gate * (_rms(proj) * pnw_ref[...])
        o_ref[rows, :] = _rms(x) * fnw_ref[...]


def _ple_final(x2d, p2d, norm_w, w_gate, b_gate, w_proj, post_w, final_w, tm):
    t = x2d.shape[0]
    return pl.pallas_call(
        _ple_final_kernel,
        grid=(t // tm,),
        in_specs=[
            pl.BlockSpec((tm, D_MODEL), lambda i: (i, 0)),
            pl.BlockSpec((tm, PLE_DIM), lambda i: (i, 0)),
            _const_spec((1, D_MODEL)),
            _const_spec((D_MODEL, D_MODEL)),
            _const_spec((1, D_MODEL)),
            _const_spec((PLE_DIM, D_MODEL)),
            _const_spec((1, D_MODEL)),
            _const_spec((1, D_MODEL)),
        ],
        out_specs=pl.BlockSpec((tm, D_MODEL), lambda i: (i, 0)),
        out_shape=jax.ShapeDtypeStruct((t, D_MODEL), F32),
        compiler_params=_params(("parallel",)),
        name="ple_final",
    )(x2d, p2d, norm_w, w_gate, b_gate, w_proj, post_w, final_w)


def _pad_cols(w, width):
    return jnp.pad(w, ((0, 0), (0, width - w.shape[1])))


def _prep_w_in(w_in):
    o_xbc = D_SSM
    o_dt = o_xbc + SSD_CONV_DIM
    o_qa = o_dt + SSD_HEADS
    o_kv = o_qa + MLA_Q_RANK
    o_kr = o_kv + MLA_KV_RANK
    w_dt = w_in[:, o_dt:o_qa]
    return jnp.concatenate([
        w_in[:, :o_dt],
        w_in[:, o_qa:o_kr],
        _pad_cols(w_in[:, o_kr:], LANES),
        _pad_cols(jnp.concatenate([w_dt] * DT_COPIES, axis=1), LANES),
    ], axis=1).astype(BF16)


def _prep_w_q_t(w_q_b):
    w = w_q_b.reshape(MLA_Q_RANK, MLA_HEADS, MLA_NOPE + MLA_ROPE)
    w = jnp.pad(w, ((0, 0), (0, 0), (0, QK_W - MLA_NOPE - MLA_ROPE)))
    return w.reshape(MLA_Q_RANK, MLA_HEADS * QK_W).T.astype(BF16)


def _prep_w_kv(w_kv_b):
    w = w_kv_b.reshape(MLA_KV_RANK, MLA_HEADS, MLA_NOPE + MLA_V)
    w_k = w[:, :, :MLA_NOPE].reshape(MLA_KV_RANK, MLA_HEADS * MLA_NOPE)
    w_v = w[:, :, MLA_NOPE:].reshape(MLA_KV_RANK, MLA_HEADS * MLA_V)
    return w_k.astype(BF16), w_v.T.astype(BF16)


def _lane_row(v, width=None):
    v = v.reshape(1, -1).astype(F32)
    return v if width is None else _pad_cols(v, width)


def kernel(x, p, positions, mix_norm_w, w_in, conv_w, conv_b, dt_bias, a_log, d_skip, ssd_norm_w,
           q_a_norm_w, w_q_b, kv_a_norm_w, w_kv_b, w_out, ffn_norm_w, w_ffn_up, ffn_conv_w,
           ffn_conv_b, w_ffn_down, ple_norm_w, w_ple_gate, b_ple_gate, w_ple_proj, ple_post_norm_w,
           final_norm_w):
    batch, seq, _ = x.shape
    depth = w_in.shape[0]
    t = batch * seq
    x2d = x.reshape(t, D_MODEL)
    c2, s2, cos_t, sin_t = _rope_tables(positions)

    assert depth == 1, "the final RMSNorm is fused into the single layer's last kernel"
    i = 0
    z, xbc, qa, ckv, kr, dt = _in_proj(x2d, _lane_row(mix_norm_w[i]), _prep_w_in(w_in[i]), tm=512)
    y_ssd = _ssd(
        z, xbc, dt, conv_w[i], _lane_row(conv_b[i]),
        _lane_row(jnp.tile(dt_bias[i], DT_COPIES), LANES),
        _lane_row(jnp.tile(a_log[i], DT_COPIES), LANES),
        _lane_row(jnp.repeat(d_skip[i], SSD_HEADDIM)), _lane_row(ssd_norm_w[i]),
        batch, seq, ts=256)
    w_k, w_v_t = _prep_w_kv(w_kv_b[i])
    q_t, k, v_t = _mla_proj(qa, ckv, kr, c2, s2, cos_t, sin_t, _lane_row(q_a_norm_w[i]),
                            _prep_w_q_t(w_q_b[i]), _lane_row(kv_a_norm_w[i]), w_k, w_v_t, tm=512)
    y_mla = _attention(q_t, k, v_t, batch, seq, tq=256)
    x2d = _out_proj(x2d, y_ssd, y_mla, w_out[i].astype(BF16), tm=512)
    x2d = _conv_ffn(x2d, _lane_row(ffn_norm_w[i]), w_ffn_up[i].astype(BF16), ffn_conv_w[i],
                    _lane_row(ffn_conv_b[i]), w_ffn_down[i].astype(BF16), seq, tm=512, tf=512)
    x2d = _ple_final(x2d, p[i].reshape(t, PLE_DIM), _lane_row(ple_norm_w[i]),
                     w_ple_gate[i].astype(BF16), _lane_row(b_ple_gate[i]), w_ple_proj[i].astype(BF16),
                     _lane_row(ple_post_norm_w[i]), _lane_row(final_norm_w), tm=512)
    return x2d.reshape(batch, seq, D_MODEL)
```

```python
import functools
import math

import jax
import jax.numpy as jnp
from jax import lax
from jax.experimental import pallas as pl
from jax.experimental.pallas import tpu as pltpu

F32 = jnp.float32
BF16 = jnp.bfloat16

D_MODEL = 2048
CHUNK = 64
D_SSM = 1024
SSD_HEADDIM = 64
SSD_HEADS = 16
SSD_GROUPS = 2
SSD_STATE = 128
SSD_CONV = 4
SSD_CONV_DIM = D_SSM + 2 * SSD_GROUPS * SSD_STATE
GROUP_W = D_SSM // SSD_GROUPS
MLA_V = 128
MLA_HEADS = 8
MLA_NOPE = 128
MLA_ROPE = 64
MLA_Q_RANK = 512
MLA_KV_RANK = 256
ROPE_THETA = 10000.0
D_FF = 5632
FFN_CONV = 3
PLE_DIM = 256
NORM_EPS = 1e-6

LANES = 128
SUBLANES = 8
QK_W = 2 * LANES
VMEM_LIMIT = 56 * 1024 * 1024
ROW_SPLIT = 2
IN_Z = 0
IN_XBC = IN_Z + D_SSM
IN_QA = IN_XBC + SSD_CONV_DIM
IN_CKV = IN_QA + MLA_Q_RANK
IN_KR = IN_CKV + MLA_KV_RANK
IN_DT = IN_KR + LANES
IN_N = IN_DT + LANES
DT_COPIES = 3


def _rms(xf):
    return xf * lax.rsqrt(jnp.mean(xf * xf, axis=-1, keepdims=True) + NORM_EPS)


def _silu(v):
    return v * jax.nn.sigmoid(v)


def _dot(a, b):
    return jnp.dot(a, b, preferred_element_type=F32)


def _dot_nt(a, b):
    return lax.dot_general(a, b, (((1,), (1,)), ((), ())), preferred_element_type=F32)


def _const_spec(shape):
    nd = len(shape)
    return pl.BlockSpec(shape, lambda *_: (0,) * nd, pipeline_mode=pl.Buffered(1))


def _params(sem, vmem=VMEM_LIMIT):
    return pltpu.CompilerParams(dimension_semantics=sem, vmem_limit_bytes=vmem)


def _rope_table_kernel(pos_ref, freq_ref, cos_ref, sin_ref):
    ang = pos_ref[...].astype(F32) * freq_ref[...]
    cos_ref[...] = jnp.cos(ang)
    sin_ref[...] = jnp.sin(ang)


def _rope_tables(positions):
    t = positions.size
    half = MLA_ROPE // 2
    inv_freq = ROPE_THETA ** (-jnp.arange(0, MLA_ROPE, 2, dtype=F32) / MLA_ROPE)
    return pl.pallas_call(
        _rope_table_kernel,
        out_shape=(jax.ShapeDtypeStruct((half, t), F32),) * 2,
        name="rope_table",
    )(positions.reshape(1, t), inv_freq.reshape(half, 1))


def _rope_t(t1, t2, cos_t, sin_t):
    return t1 * cos_t - t2 * sin_t, t2 * cos_t + t1 * sin_t


def _in_proj_kernel(x_ref, nw_ref, w_ref, z_ref, xbc_ref, qa_ref, ckv_ref, kr_ref, dt_ref):
    sub = x_ref.shape[0] // ROW_SPLIT
    for r in range(ROW_SPLIT):
        rows = slice(r * sub, (r + 1) * sub)
        h = (_rms(x_ref[rows, :]) * nw_ref[...]).astype(BF16)

        def mm(lo, hi):
            return _dot(h, w_ref[:, lo:hi])

        z_ref[rows, :] = mm(IN_Z, IN_XBC).astype(BF16)
        xbc_ref[rows, :] = mm(IN_XBC, IN_QA).astype(BF16)
        qa_ref[rows, :] = mm(IN_QA, IN_CKV).astype(BF16)
        ckv_ref[rows, :] = mm(IN_CKV, IN_KR).astype(BF16)
        kr_ref[rows, :] = mm(IN_KR, IN_DT)
        dt_ref[rows, :] = mm(IN_DT, IN_N)


def _in_proj(x2d, norm_w, w_in_r, tm):
    t = x2d.shape[0]
    widths = (D_SSM, SSD_CONV_DIM, MLA_Q_RANK, MLA_KV_RANK, LANES, LANES)
    dtypes = (BF16, BF16, BF16, BF16, F32, F32)
    return pl.pallas_call(
        _in_proj_kernel,
        grid=(t // tm,),
        in_specs=[
            pl.BlockSpec((tm, D_MODEL), lambda i: (i, 0)),
            _const_spec((1, D_MODEL)),
            _const_spec((D_MODEL, IN_N)),
        ],
        out_specs=[pl.BlockSpec((tm, w), lambda i: (i, 0)) for w in widths],
        out_shape=[jax.ShapeDtypeStruct((t, w), d) for w, d in zip(widths, dtypes)],
        compiler_params=_params(("parallel",)),
        name="in_proj",
    )(x2d, norm_w, w_in_r)


def _ssd_kernel(z_ref, xbc_ref, dt_ref, cw_ref, cb_ref, dtb_ref, alog_ref, dsk_ref, nw_ref,
                shift_ref, cum_ref, e3_ref, e1_ref, y_ref, xpad_ref, state_ref, yacc_ref, *, ts):
    blk = pl.program_id(1)
    halo = SUBLANES

    @pl.when(blk == 0)
    def _():
        xpad_ref[0:halo, :] = jnp.zeros((halo, SSD_CONV_DIM), F32)
        state_ref[...] = jnp.zeros_like(state_ref)

    x16 = xbc_ref[...]
    xpad_ref[halo:2 * halo, :] = x16[0:halo].astype(F32)
    head = cb_ref[...]
    for k in range(SSD_CONV):
        off = halo - (SSD_CONV - 1) + k
        head = head + cw_ref[k:k + 1, :] * xpad_ref[off:off + halo, :]
    xpad_ref[0:halo, :] = x16[ts - halo:ts].astype(F32)
    delayed = _dot(shift_ref[...], x16)
    body = cb_ref[...] + cw_ref[SSD_CONV - 1:SSD_CONV, :] * x16.astype(F32)
    for d in range(1, SSD_CONV):
        k = SSD_CONV - 1 - d
        body = body + cw_ref[k:k + 1, :] * delayed[(d - 1) * ts:d * ts]
    xc = _silu(jnp.concatenate([head, body[halo:]], axis=0))
    xs = xc[:, :D_SSM]
    bm = xc[:, D_SSM:D_SSM + SSD_GROUPS * SSD_STATE]
    cm = xc[:, D_SSM + SSD_GROUPS * SSD_STATE:]

    lane = lax.broadcasted_iota(jnp.int32, (ts, LANES), 1)
    raw = dt_ref[...] + dtb_ref[...]
    dt = jnp.maximum(raw, 0.0) + jnp.log1p(jnp.exp(-jnp.abs(raw)))
    a = jnp.where(lane < DT_COPIES * SSD_HEADS, -jnp.exp(alog_ref[...]), 0.0)
    adt = dt * a

    def split3(v):
        hi = v.astype(BF16).astype(F32)
        rem = v - hi
        mid = rem.astype(BF16).astype(F32)
        lo = rem - mid
        return jnp.where(lane < SSD_HEADS, hi, jnp.where(lane < 2 * SSD_HEADS, mid, lo)).astype(BF16)

    parts = _dot(cum_ref[...], split3(adt))
    cum = parts
    for shift in (SSD_HEADS, 2 * SSD_HEADS, LANES - SSD_HEADS, LANES - 2 * SSD_HEADS):
        cum = cum + pltpu.roll(parts, shift, 1)
    lane2 = lax.broadcasted_iota(jnp.int32, (2 * ts, LANES), 1)
    cum = jnp.where(lane2 < DT_COPIES * SSD_HEADS, cum, 0.0)
    acs, tot = cum[:ts], cum[ts:]
    eacs = jnp.exp(acs)
    dtdec = dt * jnp.exp(tot - acs)

    stacked = jnp.concatenate([dt, eacs, dtdec], axis=0).astype(BF16)
    expanded = _dot(stacked, e1_ref[...])
    dt_e, eacs_e, dtdec_e = expanded[:ts], expanded[ts:2 * ts], expanded[2 * ts:]

    col_part = _dot(split3(acs), e3_ref[...])

    xdt = (xs * dt_e).astype(BF16)
    xdec = (xs * dtdec_e).astype(BF16)
    bm16 = bm.astype(BF16)
    cm16 = cm.astype(BF16)

    li = lax.broadcasted_iota(jnp.int32, (CHUNK, D_SSM), 0)
    si = lax.broadcasted_iota(jnp.int32, (CHUNK, D_SSM), 1) % CHUNK
    heads_per_mm = 2 * LANES // SSD_HEADDIM
    bw = heads_per_mm * SSD_HEADDIM
    bd_r = lax.broadcasted_iota(jnp.int32, (bw, bw), 0) // CHUNK
    bd_c = lax.broadcasted_iota(jnp.int32, (bw, bw), 1) // SSD_HEADDIM
    bd_mask = bd_r == bd_c

    for c in range(ts // CHUNK):
        r0 = c * CHUNK
        rows = slice(r0, r0 + CHUNK)
        cp = col_part[rows]
        row_part = jnp.sum(jnp.where(li == si, cp, 0.0), axis=0, keepdims=True)
        lmat = jnp.exp(jnp.where(li >= si, cp - row_part, -jnp.inf))

        pieces = []
        for g in range(SSD_GROUPS):
            gl = slice(g * SSD_STATE, (g + 1) * SSD_STATE)
            gw = slice(g * GROUP_W, (g + 1) * GROUP_W)
            cg = cm16[rows, gl]
            bg = bm16[rows, gl]
            heads_g = GROUP_W // CHUNK
            cb_t = _dot_nt(cg, jnp.concatenate([bg] * heads_g, axis=0))
            m_g = (cb_t * lmat[:, gw]).astype(BF16)
            y_off = _dot(cg, state_ref[g].astype(BF16)) * eacs_e[rows, gw]
            y_diag = []
            for j in range(GROUP_W // bw):
                cols = slice(g * GROUP_W + j * bw, g * GROUP_W + (j + 1) * bw)
                xj = xdt[rows, cols]
                bd = jnp.where(bd_mask, jnp.concatenate([xj] * heads_per_mm, axis=0), jnp.zeros((), BF16))
                y_diag.append(_dot(m_g[:, j * bw:(j + 1) * bw], bd))
            pieces.append(jnp.concatenate(y_diag, axis=1) + y_off)
            bg_t = bm[rows, gl].T.astype(BF16)
            new_state = _dot(bg_t, xdec[rows, gw])
            decay = eacs_e[r0 + CHUNK - 1:r0 + CHUNK, gw]
            state_ref[g] = state_ref[g] * decay + new_state
        yacc_ref[rows, :] = jnp.concatenate(pieces, axis=1)

    y = yacc_ref[...] + dsk_ref[...] * xs
    y = y * _silu(z_ref[...].astype(F32))
    outs = []
    for g in range(SSD_GROUPS):
        outs.append(_rms(y[:, g * GROUP_W:(g + 1) * GROUP_W]))
    y_ref[...] = (jnp.concatenate(outs, axis=1) * nw_ref[...]).astype(BF16)


def _ssd(z, xbc, dt, conv_w, conv_b, dtb, alog, dsk, norm_w, batch, seq, ts):
    nblk = seq // ts
    r = jnp.arange(ts)
    same = (r[:, None] // CHUNK) == (r[None, :] // CHUNK)
    cum_mat = jnp.concatenate([same & (r[:, None] >= r[None, :]), same], axis=0).astype(BF16)
    shift_mat = jnp.concatenate([r[:, None] - d == r[None, :] for d in range(1, SSD_CONV)], axis=0).astype(BF16)
    lane_head = jnp.arange(D_SSM) // SSD_HEADDIM
    e_rows = (jnp.arange(LANES)[:, None] % SSD_HEADS) == lane_head[None, :]
    e1 = (e_rows & (jnp.arange(LANES)[:, None] < SSD_HEADS)).astype(BF16)
    e3 = (e_rows & (jnp.arange(LANES)[:, None] < DT_COPIES * SSD_HEADS)).astype(BF16)

    def row_spec(w):
        return pl.BlockSpec((ts, w), lambda b, j: (b * nblk + j, 0))

    return pl.pallas_call(
        functools.partial(_ssd_kernel, ts=ts),
        grid=(batch, nblk),
        in_specs=[
            row_spec(D_SSM), row_spec(SSD_CONV_DIM), row_spec(LANES),
            _const_spec((SSD_CONV, SSD_CONV_DIM)), _const_spec((1, SSD_CONV_DIM)),
            _const_spec((1, LANES)), _const_spec((1, LANES)),
            _const_spec((1, D_SSM)), _const_spec((1, D_SSM)),
            _const_spec(((SSD_CONV - 1) * ts, ts)),
            _const_spec((2 * ts, ts)), _const_spec((LANES, D_SSM)), _const_spec((LANES, D_SSM)),
        ],
        out_specs=row_spec(D_SSM),
        out_shape=jax.ShapeDtypeStruct((batch * seq, D_SSM), BF16),
        scratch_shapes=[
            pltpu.VMEM((2 * SUBLANES, SSD_CONV_DIM), F32),
            pltpu.VMEM((SSD_GROUPS, SSD_STATE, GROUP_W), F32),
            pltpu.VMEM((ts, D_SSM), F32),
        ],
        compiler_params=_params(("parallel", "arbitrary")),
        name="ssd",
    )(z, xbc, dt, conv_w, conv_b, dtb, alog, dsk, norm_w, shift_mat, cum_mat, e3, e1)


def _mla_proj_kernel(qa_ref, ckv_ref, kr_ref, cos_t_ref, sin_t_ref, qnw_ref, wq_t_ref,
                     kvnw_ref, wk_ref, wv_t_ref, q_t_ref, k_ref, v_t_ref):
    half = MLA_ROPE // 2
    scale = math.log2(math.e) / math.sqrt(MLA_NOPE + MLA_ROPE)
    qn = (_rms(qa_ref[...].astype(F32)) * qnw_ref[...]).astype(BF16)
    cn = (_rms(ckv_ref[...].astype(F32)) * kvnw_ref[...]).astype(BF16)
    cos_t = cos_t_ref[...]
    sin_t = sin_t_ref[...]

    q_t = _dot_nt(wq_t_ref[...], qn) * scale
    for h in range(MLA_HEADS):
        base = h * QK_W
        r1 = base + MLA_NOPE
        r2 = r1 + half
        o1, o2 = _rope_t(q_t[r1:r2], q_t[r2:r2 + half], cos_t, sin_t)
        q_t_ref[base:r1, :] = q_t[base:r1].astype(BF16)
        q_t_ref[r1:r2, :] = o1.astype(BF16)
        q_t_ref[r2:r2 + half, :] = o2.astype(BF16)
        q_t_ref[r2 + half:base + QK_W, :] = jnp.zeros((QK_W - MLA_NOPE - MLA_ROPE, q_t.shape[1]), BF16)

    kr_t = kr_ref[...].T
    o1, o2 = _rope_t(kr_t[0:half], kr_t[half:MLA_ROPE], cos_t, sin_t)
    k_rope_t = jnp.concatenate([o1, o2, jnp.zeros((LANES - MLA_ROPE, kr_t.shape[1]), F32)], axis=0)
    k_rope = k_rope_t.T.astype(BF16)
    k_nope = _dot(cn, wk_ref[...])
    for h in range(MLA_HEADS):
        base = h * QK_W
        k_ref[:, base:base + LANES] = k_nope[:, h * MLA_NOPE:(h + 1) * MLA_NOPE].astype(BF16)
        k_ref[:, base + LANES:base + QK_W] = k_rope

    v_t_ref[...] = _dot_nt(wv_t_ref[...], cn).astype(BF16)


def _mla_proj(qa, ckv, kr, cos_t, sin_t, qnw, wq_t, kvnw, wk, wv_t, tm):
    t = qa.shape[0]
    half = MLA_ROPE // 2

    def row_spec(w):
        return pl.BlockSpec((tm, w), lambda i: (i, 0))

    def col_spec(h):
        return pl.BlockSpec((h, tm), lambda i: (0, i))

    return pl.pallas_call(
        _mla_proj_kernel,
        grid=(t // tm,),
        in_specs=[
            row_spec(MLA_Q_RANK), row_spec(MLA_KV_RANK), row_spec(LANES), col_spec(half), col_spec(half),
            _const_spec((1, MLA_Q_RANK)), _const_spec((MLA_HEADS * QK_W, MLA_Q_RANK)),
            _const_spec((1, MLA_KV_RANK)), _const_spec((MLA_KV_RANK, MLA_HEADS * MLA_NOPE)),
            _const_spec((MLA_HEADS * MLA_V, MLA_KV_RANK)),
        ],
        out_specs=[col_spec(MLA_HEADS * QK_W), row_spec(MLA_HEADS * QK_W), col_spec(MLA_HEADS * MLA_V)],
        out_shape=[jax.ShapeDtypeStruct((MLA_HEADS * QK_W, t), BF16),
                   jax.ShapeDtypeStruct((t, MLA_HEADS * QK_W), BF16),
                   jax.ShapeDtypeStruct((MLA_HEADS * MLA_V, t), BF16)],
        compiler_params=_params(("parallel",)),
        name="mla_proj",
    )(qa, ckv, kr, cos_t, sin_t, qnw, wq_t, kvnw, wk, wv_t)


def _attention_kernel(q_t_ref, k_ref, v_t_ref, o_ref, s_ref, *, seq, tq):
    key_chunk = lax.broadcasted_iota(jnp.int32, (tq, tq), 0) // CHUNK
    query_chunk = lax.broadcasted_iota(jnp.int32, (tq, tq), 1) // CHUNK
    diag_mask = key_chunk <= query_chunk
    nq = seq // tq

    def score_block(i, j):
        keys = slice(j * tq, (j + 1) * tq)
        s = _dot(k_ref[keys, :], q_t_ref[:, i * tq:(i + 1) * tq])
        if j == i:
            s = jnp.where(diag_mask, s, -jnp.inf)
        s_ref[i % 2, keys, :] = s
        return jnp.max(s, axis=0, keepdims=True)

    def value_block(i, j, m):
        keys = slice(j * tq, (j + 1) * tq)
        p = jnp.exp2(s_ref[i % 2, keys, :] - m)
        return jnp.sum(p, axis=0, keepdims=True), _dot(v_t_ref[:, keys], p.astype(BF16))

    m = score_block(0, 0)
    for i in range(nq):
        denom = jnp.zeros((1, tq), F32)
        acc = jnp.zeros((MLA_V, tq), F32)
        m_next = None
        for j in range(i + 2):
            if i + 1 < nq:
                block_max = score_block(i + 1, j)
                m_next = block_max if m_next is None else jnp.maximum(m_next, block_max)
            if j <= i:
                psum, pv = value_block(i, j, m)
                denom = denom + psum
                acc = acc + pv
        o_ref[i * tq:(i + 1) * tq, :] = (acc / denom).T.astype(BF16)
        m = m_next


def _attention(q_t, k, v_t, batch, seq, tq):
    k3 = k.reshape(batch, seq, MLA_HEADS * QK_W)

    def feature_major(rows):
        return pl.BlockSpec((rows, seq), lambda b, h: (h, b))

    def token_major(w):
        return pl.BlockSpec((None, seq, w), lambda b, h: (b, 0, h))

    out = pl.pallas_call(
        functools.partial(_attention_kernel, seq=seq, tq=tq),
        grid=(batch, MLA_HEADS),
        in_specs=[feature_major(QK_W), token_major(QK_W), feature_major(MLA_V)],
        out_specs=token_major(MLA_V),
        out_shape=jax.ShapeDtypeStruct((batch, seq, MLA_HEADS * MLA_V), BF16),
        scratch_shapes=[pltpu.VMEM((2, seq, tq), F32)],
        compiler_params=_params(("parallel", "parallel")),
        name="attention",
    )(q_t, k3, v_t)
    return out.reshape(batch * seq, MLA_HEADS * MLA_V)


def _out_proj_kernel(x_ref, ys_ref, ym_ref, ws_ref, wm_ref, o_ref):
    o_ref[...] = x_ref[...] + _dot(ys_ref[...], ws_ref[...]) + _dot(ym_ref[...], wm_ref[...])


def _out_proj(x2d, y_ssd, y_mla, w_o, tm):
    t = x2d.shape[0]
    assert D_SSM == MLA_HEADS * MLA_V

    def half_spec(j):
        return pl.BlockSpec((D_SSM, D_MODEL), lambda i: (j, 0), pipeline_mode=pl.Buffered(1))

    return pl.pallas_call(
        _out_proj_kernel,
        grid=(t // tm,),
        in_specs=[
            pl.BlockSpec((tm, D_MODEL), lambda i: (i, 0)),
            pl.BlockSpec((tm, D_SSM), lambda i: (i, 0)),
            pl.BlockSpec((tm, MLA_HEADS * MLA_V), lambda i: (i, 0)),
            half_spec(0),
            half_spec(1),
        ],
        out_specs=pl.BlockSpec((tm, D_MODEL), lambda i: (i, 0)),
        out_shape=jax.ShapeDtypeStruct((t, D_MODEL), F32),
        compiler_params=_params(("parallel",)),
        name="out_proj",
    )(x2d, y_ssd, y_mla, w_o, w_o)


def _conv_ffn_kernel(x_ref, nw_ref, wg_ref, wu_ref, cwg_ref, cwu_ref, cbg_ref, cbu_ref, wd_ref, o_ref,
                     h_ref, acc_ref, gpad_ref, upad_ref, ghalo_ref, uhalo_ref, *, tm, tiles_per_seq):
    i = pl.program_id(0)
    f = pl.program_id(1)
    nf = pl.num_programs(1)
    halo = SUBLANES

    @pl.when(f == 0)
    def _():
        h_ref[...] = (_rms(x_ref[...]) * nw_ref[...]).astype(BF16)
        acc_ref[...] = jnp.zeros_like(acc_ref)

    h = h_ref[...]
    seq_start = (i % tiles_per_seq) == 0

    def conv(pad_ref, halo_ref, w_ref, cw_ref, cb_ref):
        prev = halo_ref[f]
        pad_ref[0:halo, :] = jnp.where(seq_start, jnp.zeros_like(prev), prev)
        pad_ref[halo:halo + tm, :] = _dot(h, w_ref[...])
        halo_ref[f] = pad_ref[tm:tm + halo, :]
        out = cb_ref[...]
        for k in range(FFN_CONV):
            off = halo - (FFN_CONV - 1) + k
            out = out + cw_ref[k:k + 1, :] * pad_ref[off:off + tm, :]
        return out

    gate = conv(gpad_ref, ghalo_ref, wg_ref, cwg_ref, cbg_ref)
    up = conv(upad_ref, uhalo_ref, wu_ref, cwu_ref, cbu_ref)
    act = (_silu(gate) * up).astype(BF16)
    acc_ref[...] += _dot(act, wd_ref[...])

    @pl.when(f == nf - 1)
    def _():
        o_ref[...] = x_ref[...] + acc_ref[...]


def _conv_ffn(x2d, norm_w, w_up, conv_w, conv_b, w_down, seq, tm, tf):
    t = x2d.shape[0]
    nf = D_FF // tf
    pad = pltpu.VMEM((tm + SUBLANES, tf), F32)
    carry = pltpu.VMEM((nf, SUBLANES, tf), F32)
    return pl.pallas_call(
        functools.partial(_conv_ffn_kernel, tm=tm, tiles_per_seq=seq // tm),
        grid=(t // tm, nf),
        in_specs=[
            pl.BlockSpec((tm, D_MODEL), lambda i, f: (i, 0)),
            _const_spec((1, D_MODEL)),
            pl.BlockSpec((D_MODEL, tf), lambda i, f: (0, f)),
            pl.BlockSpec((D_MODEL, tf), lambda i, f: (0, f + nf)),
            pl.BlockSpec((FFN_CONV, tf), lambda i, f: (0, f)),
            pl.BlockSpec((FFN_CONV, tf), lambda i, f: (0, f + nf)),
            pl.BlockSpec((1, tf), lambda i, f: (0, f)),
            pl.BlockSpec((1, tf), lambda i, f: (0, f + nf)),
            pl.BlockSpec((tf, D_MODEL), lambda i, f: (f, 0)),
        ],
        out_specs=pl.BlockSpec((tm, D_MODEL), lambda i, f: (i, 0)),
        out_shape=jax.ShapeDtypeStruct((t, D_MODEL), F32),
        scratch_shapes=[
            pltpu.VMEM((tm, D_MODEL), BF16),
            pltpu.VMEM((tm, D_MODEL), F32),
            pad, pad,
            carry, carry,
        ],
        compiler_params=_params(("arbitrary", "arbitrary")),
        name="conv_ffn",
    )(x2d, norm_w, w_up, w_up, conv_w, conv_w, conv_b, conv_b, w_down)


def _ple_final_kernel(x_ref, p_ref, nw_ref, wg_ref, bg_ref, wp_ref, pnw_ref, fnw_ref, o_ref):
    sub = x_ref.shape[0] // ROW_SPLIT
    for r in range(ROW_SPLIT):
        rows = slice(r * sub, (r + 1) * sub)
        x = x_ref[rows, :]
        hn = (_rms(x) * nw_ref[...]).astype(BF16)
        gate = jax.nn.sigmoid(_dot(hn, wg_ref[...]) + bg_ref[...])
        proj = _dot(p_ref[rows, :].astype(BF16), wp_ref[...])
        x = x + gate * (_rms(proj) * pnw_ref[...])
        o_ref[rows, :] = _rms(x) * fnw_ref[...]


def _ple_final(x2d, p2d, norm_w, w_gate, b_gate, w_proj, post_w, final_w, tm):
    t = x2d.shape[0]
    return pl.pallas_call(
        _ple_final_kernel,
        grid=(t // tm,),
        in_specs=[
            pl.BlockSpec((tm, D_MODEL), lambda i: (i, 0)),
            pl.BlockSpec((tm, PLE_DIM), lambda i: (i, 0)),
            _const_spec((1, D_MODEL)),
            _const_spec((D_MODEL, D_MODEL)),
            _const_spec((1, D_MODEL)),
            _const_spec((PLE_DIM, D_MODEL)),
            _const_spec((1, D_MODEL)),
            _const_spec((1, D_MODEL)),
        ],
        out_specs=pl.BlockSpec((tm, D_MODEL), lambda i: (i, 0)),
        out_shape=jax.ShapeDtypeStruct((t, D_MODEL), F32),
        compiler_params=_params(("parallel",)),
        name="ple_final",
    )(x2d, p2d, norm_w, w_gate, b_gate, w_proj, post_w, final_w)


def _pad_cols(w, width):
    return jnp.pad(w, ((0, 0), (0, width - w.shape[1])))


def _prep_w_in(w_in):
    o_xbc = D_SSM
    o_dt = o_xbc + SSD_CONV_DIM
    o_qa = o_dt + SSD_HEADS
    o_kv = o_qa + MLA_Q_RANK
    o_kr = o_kv + MLA_KV_RANK
    w_dt = w_in[:, o_dt:o_qa]
    return jnp.concatenate([
        w_in[:, :o_dt],
        w_in[:, o_qa:o_kr],
        _pad_cols(w_in[:, o_kr:], LANES),
        _pad_cols(jnp.concatenate([w_dt] * DT_COPIES, axis=1), LANES),
    ], axis=1).astype(BF16)


def _prep_w_q_t(w_q_b):
    w = w_q_b.reshape(MLA_Q_RANK, MLA_HEADS, MLA_NOPE + MLA_ROPE)
    w = jnp.pad(w, ((0, 0), (0, 0), (0, QK_W - MLA_NOPE - MLA_ROPE)))
    return w.reshape(MLA_Q_RANK, MLA_HEADS * QK_W).T.astype(BF16)


def _prep_w_kv(w_kv_b):
    w = w_kv_b.reshape(MLA_KV_RANK, MLA_HEADS, MLA_NOPE + MLA_V)
    w_k = w[:, :, :MLA_NOPE].reshape(MLA_KV_RANK, MLA_HEADS * MLA_NOPE)
    w_v = w[:, :, MLA_NOPE:].reshape(MLA_KV_RANK, MLA_HEADS * MLA_V)
    return w_k.astype(BF16), w_v.T.astype(BF16)


def _lane_row(v, width=None):
    v = v.reshape(1, -1).astype(F32)
    return v if width is None else _pad_cols(v, width)


def kernel(x, p, positions, mix_norm_w, w_in, conv_w, conv_b, dt_bias, a_log, d_skip, ssd_norm_w,
           q_a_norm_w, w_q_b, kv_a_norm_w, w_kv_b, w_out, ffn_norm_w, w_ffn_up, ffn_conv_w,
           ffn_conv_b, w_ffn_down, ple_norm_w, w_ple_gate, b_ple_gate, w_ple_proj, ple_post_norm_w,
           final_norm_w):
    batch, seq, _ = x.shape
    depth = w_in.shape[0]
    t = batch * seq
    x2d = x.reshape(t, D_MODEL)
    cos_t, sin_t = _rope_tables(positions)

    assert depth == 1, "the final RMSNorm is fused into the single layer's last kernel"
    i = 0
    z, xbc, qa, ckv, kr, dt = _in_proj(x2d, _lane_row(mix_norm_w[i]), _prep_w_in(w_in[i]), tm=512)
    y_ssd = _ssd(
        z, xbc, dt, conv_w[i], _lane_row(conv_b[i]),
        _lane_row(jnp.tile(dt_bias[i], DT_COPIES), LANES),
        _lane_row(jnp.tile(a_log[i], DT_COPIES), LANES),
        _lane_row(jnp.repeat(d_skip[i], SSD_HEADDIM)), _lane_row(ssd_norm_w[i]),
        batch, seq, ts=256)
    w_k, w_v_t = _prep_w_kv(w_kv_b[i])
    q_t, k, v_t = _mla_proj(qa, ckv, kr, cos_t, sin_t, _lane_row(q_a_norm_w[i]),
                            _prep_w_q_t(w_q_b[i]), _lane_row(kv_a_norm_w[i]), w_k, w_v_t, tm=512)
    y_mla = _attention(q_t, k, v_t, batch, seq, tq=256)
    x2d = _out_proj(x2d, y_ssd, y_mla, w_out[i].astype(BF16), tm=512)
    x2d = _conv_ffn(x2d, _lane_row(ffn_norm_w[i]), w_ffn_up[i].astype(BF16), ffn_conv_w[i],
                    _lane_row(ffn_conv_b[i]), w_ffn_down[i].astype(BF16), seq, tm=512, tf=512)
    x2d = _ple_final(x2d, p[i].reshape(t, PLE_DIM), _lane_row(ple_norm_w[i]),
                     w_ple_gate[i].astype(BF16), _lane_row(b_ple_gate[i]), w_ple_proj[i].astype(BF16),
                     _lane_row(ple_post_norm_w[i]), _lane_row(final_norm_w), tm=512)
    return x2d.reshape(batch, seq, D_MODEL)
```
